```python
import math
import jax
import jax.numpy as jnp
from jax import lax
import numpy as np

D_MODEL = 1024
BATCH = 16
SEQ = 2048
DEPTH = 2

GRID_W = 64
CTX_LEN = 256
NA_HEADS = 8
HEAD_DIM = 64
NA_WIDTH = NA_HEADS * HEAD_DIM
WIN_H_MAX = 8
WIN_W = 16
QBLK_W = 16
KBLK_W = QBLK_W + WIN_W
ROPE_THETA = 100.0
HY_WIDTH = 512
HY_ORDER = 2
HY_SHORT = 3
HY_BANDS = 16
HY_EMB = 1 + 2 * HY_BANDS
HY_FILTER_HIDDEN = 64
HY_SIN_FREQ = 1.0
HY_MAX_DECAY = math.log(1e-2) / 0.3
HY_MIN_DECAY = math.log(1e-2) / 1.5
N_BRANCH = 2
IN_COLS = 3 * NA_WIDTH + (HY_ORDER + 1) * HY_WIDTH + N_BRANCH * D_MODEL
N_GROUPS = 4
EXPERTS_PER_GROUP = 8
N_EXPERTS = N_GROUPS * EXPERTS_PER_GROUP
TOP_K = 2
D_EXPERT = 256
EPS = 1e-6
F32 = jnp.float32

kernel_name = 'hybrid_natten_hyena_hmoe_block'


def _rms(x, g):
    x32 = x.astype(F32)
    y = x32 * lax.rsqrt(jnp.mean(x32 * x32, axis=-1, keepdims=True) + EPS)
    return (y * g.astype(F32)).astype(x.dtype)


def _rope_2d(x, rows, cols):
    quarter = x.shape[-1] // 4
    freqs = ROPE_THETA ** (-jnp.arange(quarter, dtype=F32) / quarter)

    def rot(xh, pos):
        ang = pos.astype(F32)[:, None] * freqs[None, :]
        cos = jnp.cos(ang)[None, :, None, :]
        sin = jnp.sin(ang)[None, :, None, :]
        a, b = jnp.split(xh.astype(F32), 2, axis=-1)
        return jnp.concatenate([a * cos - b * sin, b * cos + a * sin], axis=-1)

    xr, xcol = jnp.split(x, 2, axis=-1)
    return jnp.concatenate([rot(xr, rows), rot(xcol, cols)], axis=-1).astype(x.dtype)


def _heads(t):
    return t.reshape(t.shape[0], t.shape[1], NA_HEADS, HEAD_DIM)


def _project(h, w_in, q_norm, k_norm):
    p = h @ w_in
    q, k, v, hy, gates = jnp.split(
        p, [NA_WIDTH, 2 * NA_WIDTH, 3 * NA_WIDTH, 3 * NA_WIDTH + (HY_ORDER + 1) * HY_WIDTH], axis=-1)
    return _rms(_heads(q), q_norm), _rms(_heads(k), k_norm), _heads(v), hy, gates


def _neighbourhood_attention(q_rot, k_rot, v, q_plain, k_ctx, v_ctx, rpb):
    B, L, H, dh = q_rot.shape
    R = L // GRID_W
    kh = min(WIN_H_MAX, R)
    nj = GRID_W // QBLK_W
    scale = dh ** -0.5
    rr = jnp.arange(R)
    row_idx = jnp.clip(rr - kh // 2, 0, R - kh)[:, None] + jnp.arange(kh)[None, :]
    jj = jnp.arange(nj)
    col_idx = jnp.clip(jj * QBLK_W - WIN_W // 2, 0, GRID_W - KBLK_W)[:, None] + jnp.arange(KBLK_W)[None, :]
    qcol = jj[:, None] * QBLK_W + jnp.arange(QBLK_W)[None, :]
    wstart = jnp.clip(qcol - WIN_W // 2, 0, GRID_W - WIN_W)
    kcol = col_idx[:, None, :]
    in_win = (kcol >= wstart[..., None]) & (kcol < wstart[..., None] + WIN_W)
    dr = row_idx - rr[:, None]
    dc = jnp.clip(kcol - qcol[..., None], -(WIN_W - 1), WIN_W - 1)
    bias = rpb[:, dr[:, None, None, :, None] + (WIN_H_MAX - 1), dc[None, :, :, None, :] + (WIN_W - 1)]
    bias = jnp.where(in_win[None, None, :, :, None, :], bias.astype(F32), -jnp.inf)

    def band(t):
        g = t.reshape(B, R, GRID_W, H, dh)[:, row_idx]
        return jnp.take(g, col_idx, axis=3)

    kb, vb = band(k_rot), band(v)
    qg = q_rot.reshape(B, R, nj, QBLK_W, H, dh)
    qp = q_plain.reshape(B, R, nj, QBLK_W, H, dh)
    s_win = jnp.einsum('brjqhd,brajkhd->bhrjqak', qg, kb, preferred_element_type=F32) * scale + bias[None]
    s_ctx = jnp.einsum('brjqhd,bchd->bhrjqc', qp, k_ctx, preferred_element_type=F32) * scale
    n_win = kh * KBLK_W
    s = jnp.concatenate([s_win.reshape(B, H, R, nj, QBLK_W, n_win), s_ctx], axis=-1)
    p = jax.nn.softmax(s, axis=-1)
    p_win = p[..., :n_win].reshape(B, H, R, nj, QBLK_W, kh, KBLK_W).astype(v.dtype)
    p_ctx = p[..., n_win:].astype(v.dtype)
    o = (jnp.einsum('bhrjqak,brajkhd->brjqhd', p_win, vb)
         + jnp.einsum('bhrjqc,bchd->brjqhd', p_ctx, v_ctx))
    return o.reshape(B, L, H * dh)


def _ctx_attention(q, k, v):
    B, L, H, dh = q.shape
    s = jnp.einsum('bqhd,bkhd->bhqk', q, k, preferred_element_type=F32) * (dh ** -0.5)
    p = jax.nn.softmax(s, axis=-1).astype(v.dtype)
    return jnp.einsum('bhqk,bkhd->bqhd', p, v).reshape(B, L, H * dh)


def _hyena_filter_fft(L, w1, b1, w2, b2, w3):
    pos = jnp.arange(L, dtype=F32)
    t = pos / max(L - 1, 1)
    w = 2.0 * math.pi * pos / L
    f = jnp.linspace(1e-4, HY_BANDS - 1, HY_BANDS, dtype=F32)
    z = jnp.concatenate([t[:, None], jnp.cos(f[None, :] * w[:, None]), -jnp.sin(f[None, :] * w[:, None])], axis=-1)
    hid = jnp.sin(HY_SIN_FREQ * (z @ w1 + b1))
    hid = jnp.sin(HY_SIN_FREQ * (hid @ w2 + b2))
    h = (hid @ w3).astype(F32).reshape(L, 2, HY_ORDER, HY_WIDTH)
    deltas = jnp.abs(jnp.linspace(HY_MIN_DECAY, HY_MAX_DECAY, HY_WIDTH, dtype=F32))
    h = h * jnp.exp(-t[:, None] * deltas[None, :])[:, None, None, :]
    h_fwd = h[:, 0]
    h_bwd = h[1:, 1]
    norm = jnp.sum(jnp.abs(h_fwd), axis=0) + jnp.sum(jnp.abs(h_bwd), axis=0) + EPS
    k2 = jnp.concatenate([h_fwd, jnp.zeros((1, HY_ORDER, HY_WIDTH), F32), h_bwd[::-1]], axis=0) / norm
    return jnp.fft.rfft(k2, axis=0)


def _short_conv(u, w, b):
    L = u.shape[1]
    pad = HY_SHORT // 2
    up = jnp.pad(u, ((0, 0), (pad, pad), (0, 0)))
    return sum(up[:, i:i + L] * w[i] for i in range(HY_SHORT)) + b


def _hyena(u, short_w, short_b, filt_fft, skip):
    L = u.shape[1]
    u = _short_conv(u, short_w, short_b)
    v, x1, x2 = jnp.split(u, 3, axis=-1)
    z = v.astype(F32)
    for o, gate in enumerate((x1, x2)):
        zf = jnp.fft.rfft(z, n=2 * L, axis=1)
        conv = jnp.fft.irfft(zf * filt_fft[None, :, o, :], n=2 * L, axis=1)[:, :L]
        z = gate.astype(F32) * (conv + z * skip[o].astype(F32))
    return z.astype(u.dtype)


def _merge(attn, hyena, gates, w_br_a, w_br_b, w_out):
    ga, gb = jnp.split(gates, 2, axis=-1)
    y = jax.nn.sigmoid(ga) * (attn @ w_br_a) + jax.nn.sigmoid(gb) * (hyena @ w_br_b)
    return y @ w_out


def _hier_moe(h, w_group, b_group, w_router, b_router, w1, w3, w2):
    T = h.shape[0]
    glog = (h @ w_group + b_group).astype(F32)
    gprob = jax.nn.softmax(glog, axis=-1)
    gidx = jnp.argmax(glog, axis=-1)
    gp = jnp.take_along_axis(gprob, gidx[:, None], axis=1)[:, 0]
    elog = (h @ w_router + b_router).astype(F32).reshape(T, N_GROUPS, EXPERTS_PER_GROUP)
    elog = jnp.take_along_axis(elog, gidx[:, None, None], axis=1)[:, 0]
    top_v, top_i = lax.top_k(elog, TOP_K)
    wts = gp[:, None] * jax.nn.softmax(top_v, axis=-1)
    eid = gidx[:, None] * EXPERTS_PER_GROUP + top_i
    cw = jnp.einsum('tk,tke->te', wts, jax.nn.one_hot(eid, N_EXPERTS, dtype=F32))
    cw = cw.reshape(T, N_GROUPS, EXPERTS_PER_GROUP).astype(h.dtype)
    y = jnp.zeros(h.shape, F32)
    for g in range(N_GROUPS):
        a = jnp.einsum('td,edf->tef', h, w1[g])
        b = jnp.einsum('td,edf->tef', h, w3[g])
        hid = jax.nn.silu(a) * b * cw[:, g, :, None]
        y = y + jnp.einsum('tef,efd->td', hid, w2[g], preferred_element_type=F32)
    return y.astype(h.dtype)


def _layer(x, xc, c, c_ctx, last, ada_w, ada_b, norm_mix, norm_ffn, w_in, q_norm, k_norm, rpb,
           short_w, short_b, flt_w1, flt_b1, flt_w2, flt_b2, flt_w3, hy_skip, w_br_a, w_br_b, w_out,
           w_group, b_group, w_router, b_router, moe_w1, moe_w3, moe_w2):
    B, L, D = x.shape
    mod = jax.nn.silu(c) @ ada_w + ada_b
    modc = jax.nn.silu(c_ctx) @ ada_w + ada_b
    sh1, sc1, g1, sh2, sc2, g2 = [m[:, None, :] for m in jnp.split(mod, 6, axis=-1)]
    sh1c, sc1c, g1c, sh2c, sc2c, g2c = jnp.split(modc, 6, axis=-1)

    h = _rms(x, norm_mix) * (1.0 + sc1) + sh1
    hc = _rms(xc, norm_mix) * (1.0 + sc1c) + sh1c
    q, k, v, hy, gates = _project(h, w_in, q_norm, k_norm)
    if last:
        kc_, vc_ = jnp.split(hc @ w_in[:, NA_WIDTH:3 * NA_WIDTH], 2, axis=-1)
        kc, vc = _rms(_heads(kc_), k_norm), _heads(vc_)
    else:
        qc, kc, vc, hyc, gatesc = _project(hc, w_in, q_norm, k_norm)
    pos = jnp.arange(L)
    rows, cols = pos // GRID_W, pos % GRID_W
    attn = _neighbourhood_attention(_rope_2d(q, rows, cols), _rope_2d(k, rows, cols), v, q, kc, vc, rpb)
    hyena = _hyena(hy, short_w, short_b, _hyena_filter_fft(L, flt_w1, flt_b1, flt_w2, flt_b2, flt_w3), hy_skip)
    x = x + g1 * _merge(attn, hyena, gates, w_br_a, w_br_b, w_out)

    h2 = _rms(x, norm_ffn) * (1.0 + sc2) + sh2
    x = x + g2 * _hier_moe(h2.reshape(B * L, D), w_group, b_group, w_router, b_router,
                           moe_w1, moe_w3, moe_w2).reshape(B, L, D)
    if last:
        return x, xc

    Lc = xc.shape[1]
    attn_c = _ctx_attention(qc, kc, vc)
    hyena_c = _hyena(hyc, short_w, short_b, _hyena_filter_fft(Lc, flt_w1, flt_b1, flt_w2, flt_b2, flt_w3), hy_skip)
    xc = xc + g1c * _merge(attn_c, hyena_c, gatesc, w_br_a, w_br_b, w_out)
    h2c = _rms(xc, norm_ffn) * (1.0 + sc2c) + sh2c
    xc = xc + g2c * _hier_moe(h2c.reshape(B * Lc, D), w_group, b_group, w_router, b_router,
                              moe_w1, moe_w3, moe_w2).reshape(B, Lc, D)
    return x, xc


def setup_inputs(seed: int = 0) -> dict:
    key = jax.random.key(seed)
    ks = list(jax.random.split(key, 40))
    cnt = [0]

    def nrm(shape, scale):
        k = ks[cnt[0]]
        cnt[0] += 1
        return scale * jax.random.normal(k, shape, F32)

    D = D_MODEL
    G, E, F = N_GROUPS, EXPERTS_PER_GROUP, D_EXPERT
    return {
        'x': nrm((BATCH, SEQ, D), 1.0),
        'c': nrm((BATCH, D), 1.0),
        'ctx': nrm((BATCH, CTX_LEN, D), 1.0),
        'c_ctx': nrm((D,), 1.0),
        'ada_w': nrm((DEPTH, D, 6 * D), 0.5 * D ** -0.5),
        'ada_b': nrm((DEPTH, 6 * D), 0.02),
        'norm_mix': 1.0 + nrm((DEPTH, D), 0.05),
        'norm_ffn': 1.0 + nrm((DEPTH, D), 0.05),
        'w_in': nrm((DEPTH, D, IN_COLS), D ** -0.5),
        'q_norm': 1.0 + nrm((DEPTH, HEAD_DIM), 0.05),
        'k_norm': 1.0 + nrm((DEPTH, HEAD_DIM), 0.05),
        'rpb': nrm((DEPTH, NA_HEADS, 2 * WIN_H_MAX - 1, 2 * WIN_W - 1), 0.1),
        'short_w': nrm((DEPTH, HY_SHORT, (HY_ORDER + 1) * HY_WIDTH), 0.5),
        'short_b': nrm((DEPTH, (HY_ORDER + 1) * HY_WIDTH), 0.02),
        'flt_w1': nrm((DEPTH, HY_EMB, HY_FILTER_HIDDEN), HY_EMB ** -0.5),
        'flt_b1': nrm((DEPTH, HY_FILTER_HIDDEN), 0.1),
        'flt_w2': nrm((DEPTH, HY_FILTER_HIDDEN, HY_FILTER_HIDDEN), HY_FILTER_HIDDEN ** -0.5),
        'flt_b2': nrm((DEPTH, HY_FILTER_HIDDEN), 0.1),
        'flt_w3': nrm((DEPTH, HY_FILTER_HIDDEN, 2 * HY_ORDER * HY_WIDTH), HY_FILTER_HIDDEN ** -0.5),
        'hy_skip': nrm((DEPTH, HY_ORDER, HY_WIDTH), 1.0),
        'w_br_a': nrm((DEPTH, NA_WIDTH, D), NA_WIDTH ** -0.5),
        'w_br_b': nrm((DEPTH, HY_WIDTH, D), HY_WIDTH ** -0.5),
        'w_out': nrm((DEPTH, D, D), D ** -0.5),
        'w_group': nrm((DEPTH, D, G), D ** -0.5),
        'b_group': nrm((DEPTH, G), 0.01),
        'w_router': nrm((DEPTH, D, G * E), D ** -0.5),
        'b_router': nrm((DEPTH, G * E), 0.01),
        'moe_w1': nrm((DEPTH, G, E, D, F), D ** -0.5),
        'moe_w3': nrm((DEPTH, G, E, D, F), D ** -0.5),
        'moe_w2': nrm((DEPTH, G, E, F, D), F ** -0.5),
    }


def reference(x, c, ctx, c_ctx, ada_w, ada_b, norm_mix, norm_ffn, w_in, q_norm, k_norm, rpb,
              short_w, short_b, flt_w1, flt_b1, flt_w2, flt_b2, flt_w3, hy_skip, w_br_a, w_br_b, w_out,
              w_group, b_group, w_router, b_router, moe_w1, moe_w3, moe_w2):
    xc = ctx
    for i in range(DEPTH):
        x, xc = _layer(x, xc, c, c_ctx, i == DEPTH - 1, ada_w[i], ada_b[i], norm_mix[i], norm_ffn[i],
                       w_in[i], q_norm[i], k_norm[i], rpb[i], short_w[i], short_b[i],
                       flt_w1[i], flt_b1[i], flt_w2[i], flt_b2[i], flt_w3[i], hy_skip[i],
                       w_br_a[i], w_br_b[i], w_out[i], w_group[i], b_group[i], w_router[i], b_router[i],
                       moe_w1[i], moe_w3[i], moe_w2[i])
    return x
```

```python
import functools
import math

import jax
import jax.numpy as jnp
from jax import lax
from jax.experimental import pallas as pl
from jax.experimental.pallas import tpu as pltpu

F32 = jnp.float32
BF16 = jnp.bfloat16

D_MODEL = 1024
GRID_W = 64
NA_HEADS = 8
HEAD_DIM = 64
NA_WIDTH = NA_HEADS * HEAD_DIM
WIN_H = 8
WIN_W = 16
ROPE_THETA = 100.0
HY_WIDTH = 512
HY_ORDER = 2
HY_BANDS = 16
HY_SIN_FREQ = 1.0
HY_MAX_DECAY = math.log(1e-2) / 0.3
HY_MIN_DECAY = math.log(1e-2) / 1.5
N_GROUPS = 4
EXPERTS_PER_GROUP = 8
N_EXPERTS = N_GROUPS * EXPERTS_PER_GROUP
D_EXPERT = 256
EPS = 1e-6
NEG = -1e30

LANES = 128
V7X_VMEM_BYTES = 64 * 1024 * 1024
Q_ROWS = 8
K_ROWS = 16


def _cparams(sem, vmem_mb):
    assert vmem_mb * 1024 * 1024 < V7X_VMEM_BYTES
    return pltpu.CompilerParams(dimension_semantics=sem, vmem_limit_bytes=vmem_mb * 1024 * 1024)


def _bdot(a, b):
    return jnp.dot(a, b, preferred_element_type=F32)


def _split(a):
    hi = a.astype(BF16)
    lo = (a - hi.astype(F32)).astype(BF16)
    return hi, lo


def _dot3(a, b):
    ah, al = _split(a)
    bh, bl = _split(b)
    return _bdot(ah, bh) + _bdot(ah, bl) + _bdot(al, bh)


def _sigmoid(x):
    return 1.0 / (1.0 + jnp.exp(-x))


def _rms_mod(x, g, sc, sh):
    ms = jnp.mean(x * x, axis=-1, keepdims=True)
    return (x * lax.rsqrt(ms + EPS) * g) * (1.0 + sc) + sh


def _ada_kernel(c_ref, w_ref, b_ref, o_ref):
    c = c_ref[...]
    o_ref[...] = _dot3(c * _sigmoid(c), w_ref[...]) + b_ref[...]


def _ada(cs, w, b):
    R, D = cs.shape
    N = w.shape[1]
    tn = 512
    return pl.pallas_call(
        _ada_kernel,
        grid=(N // tn,),
        in_specs=[pl.BlockSpec((R, D), lambda j: (0, 0)),
                  pl.BlockSpec((D, tn), lambda j: (0, j)),
                  pl.BlockSpec((1, tn), lambda j: (0, j))],
        out_specs=pl.BlockSpec((R, tn), lambda j: (0, j)),
        out_shape=jax.ShapeDtypeStruct((R, N), F32),
        compiler_params=_cparams(("parallel",), 32),
        name="ada_mod",
    )(cs, w, b.reshape(1, N))


def _nmm_kernel(x_ref, g_ref, sc_ref, sh_ref, w_ref, o_ref, h_ref):
    @pl.when(pl.program_id(2) == 0)
    def _():
        h_ref[...] = _rms_mod(x_ref[0], g_ref[...], sc_ref[0], sh_ref[0]).astype(BF16)

    o_ref[0] = _bdot(h_ref[...], w_ref[...]).astype(o_ref.dtype)


def _norm_mod_matmul(x, g, sc, sh, w):
    Bx, Lx, D = x.shape
    N = w.shape[1]
    tm = min(Lx, 512)
    tn = 1280 if N % 1280 == 0 else 1024
    return pl.pallas_call(
        _nmm_kernel,
        grid=(Bx, Lx // tm, N // tn),
        in_specs=[pl.BlockSpec((1, tm, D), lambda b, i, j: (b, i, 0)),
                  pl.BlockSpec((1, D), lambda b, i, j: (0, 0)),
                  pl.BlockSpec((1, 1, D), lambda b, i, j: (b, 0, 0)),
                  pl.BlockSpec((1, 1, D), lambda b, i, j: (b, 0, 0)),
                  pl.BlockSpec((D, tn), lambda b, i, j: (0, j))],
        out_specs=pl.BlockSpec((1, tm, tn), lambda b, i, j: (b, i, j)),
        out_shape=jax.ShapeDtypeStruct((Bx, Lx, N), F32),
        scratch_shapes=[pltpu.VMEM((tm, D), BF16)],
        compiler_params=_cparams(("parallel", "parallel", "arbitrary"), 40),
        name="norm_mod_proj",
    )(x, g.reshape(1, D), sc, sh, w)


def _head_norm(x, gn, bd):
    hi, lo = _split(x * x)
    ms = _bdot(hi, bd) + _bdot(lo, bd)
    return x * lax.rsqrt(ms + EPS) * gn


def _rope(x, cos, sin_signed):
    lane = lax.broadcasted_iota(jnp.int32, x.shape, 1)
    quarter = HEAD_DIM // 4
    partner = jnp.where((lane % (2 * quarter)) < quarter,
                        pltpu.roll(x, x.shape[1] - quarter, 1), pltpu.roll(x, quarter, 1))
    return x * cos + partner * sin_signed


def _qk_rope_kernel(q_ref, k_ref, qn_ref, kn_ref, cos_ref, sin_ref, bd_ref, qr_ref, qp_ref, kr_ref):
    bd = bd_ref[...]
    cos = cos_ref[...]
    sin = sin_ref[...]
    q = _head_norm(q_ref[0], qn_ref[...], bd) * (HEAD_DIM ** -0.5)
    k = _head_norm(k_ref[0], kn_ref[...], bd)
    qp_ref[0] = q.astype(BF16)
    qr_ref[0] = _rope(q, cos, sin).astype(BF16)
    kr_ref[0] = _rope(k, cos, sin).astype(BF16)


def _qk_plain_kernel(q_ref, k_ref, qn_ref, kn_ref, bd_ref, qp_ref, kp_ref):
    bd = bd_ref[...]
    qp_ref[0] = (_head_norm(q_ref[0], qn_ref[...], bd) * (HEAD_DIM ** -0.5)).astype(BF16)
    kp_ref[0] = _head_norm(k_ref[0], kn_ref[...], bd).astype(BF16)


def _head_block_diag():
    r = jnp.arange(NA_WIDTH) // HEAD_DIM
    return jnp.where(r[:, None] == r[None, :], 1.0 / HEAD_DIM, 0.0).astype(BF16)


def _qk_prep(p, qcol, kcol, qn, kn, rope_tabs):
    Bx, Lx, _ = p.shape
    W = NA_WIDTH
    tm = min(Lx, 512)
    tok = lambda c: pl.BlockSpec((1, tm, W), lambda b, i: (b, i, c))
    vec = pl.BlockSpec((1, W), lambda b, i: (0, 0))
    mat = pl.BlockSpec((W, W), lambda b, i: (0, 0))
    out = pl.BlockSpec((1, tm, W), lambda b, i: (b, i, 0))
    osd = jax.ShapeDtypeStruct((Bx, Lx, W), BF16)
    qn_t = jnp.tile(qn, NA_HEADS).reshape(1, W)
    kn_t = jnp.tile(kn, NA_HEADS).reshape(1, W)
    bd = _head_block_diag()
    if rope_tabs is None:
        return pl.pallas_call(
            _qk_plain_kernel, grid=(Bx, Lx // tm),
            in_specs=[tok(qcol), tok(kcol), vec, vec, mat],
            out_specs=[out, out], out_shape=[osd, osd],
            compiler_params=_cparams(("parallel", "parallel"), 32), name="qk_norm",
        )(p, p, qn_t, kn_t, bd)
    cos, sin = rope_tabs
    tab = pl.BlockSpec((tm, W), lambda b, i: (i, 0))
    return pl.pallas_call(
        _qk_rope_kernel, grid=(Bx, Lx // tm),
        in_specs=[tok(qcol), tok(kcol), vec, vec, tab, tab, mat],
        out_specs=[out, out, out], out_shape=[osd, osd, osd],
        compiler_params=_cparams(("parallel", "parallel"), 32), name="qk_norm_rope",
    )(p, p, qn_t, kn_t, cos, sin, bd)


def _rope_tables(L):
    quarter = HEAD_DIM // 4
    freqs = ROPE_THETA ** (-jnp.arange(quarter, dtype=F32) / quarter)
    pos = jnp.arange(L)
    rows, cols = (pos // GRID_W).astype(F32), (pos % GRID_W).astype(F32)
    d = jnp.arange(NA_WIDTH) % HEAD_DIM
    p = jnp.where((d < HEAD_DIM // 2)[None, :], rows[:, None], cols[:, None])
    ang = p * freqs[d % quarter][None, :]
    sign = jnp.where((d % (2 * quarter)) < quarter, -1.0, 1.0)[None, :]
    return jnp.cos(ang), jnp.sin(ang) * sign


def _softmax_pv(s_list, v_list):
    m = s_list[0].max(axis=-1, keepdims=True)
    for s in s_list[1:]:
        m = jnp.maximum(m, s.max(axis=-1, keepdims=True))
    l = 0.0
    o = 0.0
    for s, v in zip(s_list, v_list):
        p = jnp.exp(s - m)
        l = l + p.sum(axis=-1, keepdims=True)
        o = o + _bdot(p.astype(BF16), v)
    return o / l


def _qkt(q, k):
    return lax.dot_general(q, k, (((1,), (1,)), ((), ())), preferred_element_type=F32)


def _head_lanes(x, lane, hh):
    return jnp.where((lane // HEAD_DIM) == hh, x, 0.0).astype(BF16)


def _nbr_attn_kernel(q_ref, qp_ref, k_ref, v_ref, kc_ref, vc_ref, bias_ref, o_ref):
    i = pl.program_id(1)
    max_row0 = k_ref.shape[1] // GRID_W - K_ROWS
    k0 = pl.multiple_of(jnp.clip(Q_ROWS * i - WIN_H // 2, 0, max_row0) * GRID_W, 4 * GRID_W)
    nk = K_ROWS * GRID_W
    k = k_ref[0, pl.ds(k0, nk), :]
    v = v_ref[0, pl.ds(k0, nk), :].astype(BF16)
    kc = kc_ref[0]
    vc = vc_ref[0].astype(BF16)
    q = q_ref[0].astype(F32)
    qp = qp_ref[0].astype(F32)
    lane = lax.broadcasted_iota(jnp.int32, q.shape, 1)
    outs = []
    for hh in range(LANES // HEAD_DIM):
        s_w = _qkt(_head_lanes(q, lane, hh), k) + bias_ref[hh, 0]
        s_c = _qkt(_head_lanes(qp, lane, hh), kc)
        outs.append(_softmax_pv([s_w, s_c], [v, vc]))
    o_ref[0] = jnp.where(lane < HEAD_DIM, outs[0], outs[1]).astype(o_ref.dtype)


def _nbr_attention(q_rot, q_plain, k_rot, p, v_blk, kc, pc, vc_blk, bias):
    B, L, _ = q_rot.shape
    Lc = kc.shape[1]
    hp_n = NA_WIDTH // LANES
    nq = Q_ROWS * GRID_W
    ni = L // nq
    heads_per = LANES // HEAD_DIM
    return pl.pallas_call(
        _nbr_attn_kernel,
        grid=(hp_n, ni, B),
        in_specs=[pl.BlockSpec((1, nq, LANES), lambda h, i, b: (b, i, h)),
                  pl.BlockSpec((1, nq, LANES), lambda h, i, b: (b, i, h)),
                  pl.BlockSpec((1, L, LANES), lambda h, i, b: (b, 0, h)),
                  pl.BlockSpec((1, L, LANES), lambda h, i, b: (b, 0, v_blk + h)),
                  pl.BlockSpec((1, Lc, LANES), lambda h, i, b: (b, 0, h)),
                  pl.BlockSpec((1, Lc, LANES), lambda h, i, b: (b, 0, vc_blk + h)),
                  pl.BlockSpec((heads_per, 1, nq, K_ROWS * GRID_W), lambda h, i, b: (h, i, 0, 0))],
        out_specs=pl.BlockSpec((1, nq, LANES), lambda h, i, b: (b, i, h)),
        out_shape=jax.ShapeDtypeStruct((B, L, NA_WIDTH), BF16),
        compiler_params=_cparams(("parallel", "parallel", "arbitrary"), 48),
        name="nbr_attention",
    )(q_rot, q_plain, k_rot, p, kc, pc, bias)


def _ctx_attn_kernel(q_ref, k_ref, v_ref, o_ref):
    q = q_ref[0].astype(F32)
    k = k_ref[0]
    v = v_ref[0].astype(BF16)
    lane = lax.broadcasted_iota(jnp.int32, q.shape, 1)
    outs = []
    for hh in range(LANES // HEAD_DIM):
        outs.append(_softmax_pv([_qkt(_head_lanes(q, lane, hh), k)], [v]))
    o_ref[0] = jnp.where(lane < HEAD_DIM, outs[0], outs[1]).astype(o_ref.dtype)


def _ctx_attention(qc, kc, pc, vc_blk):
    B, Lc, _ = qc.shape
    blk = lambda off: pl.BlockSpec((1, Lc, LANES), lambda h, b: (b, 0, off + h))
    return pl.pallas_call(
        _ctx_attn_kernel,
        grid=(NA_WIDTH // LANES, B),
        in_specs=[blk(0), blk(0), blk(vc_blk)],
        out_specs=blk(0),
        out_shape=jax.ShapeDtypeStruct((B, Lc, NA_WIDTH), BF16),
        compiler_params=_cparams(("parallel", "parallel"), 32),
        name="ctx_attention",
    )(qc, kc, pc)


def _nbr_bias(rpb, L):
    R = L // GRID_W
    kh = min(WIN_H, R)
    qc = jnp.arange(GRID_W)[:, None]
    kcol = jnp.arange(GRID_W)[None, :]
    wstart = jnp.clip(qc - WIN_W // 2, 0, GRID_W - WIN_W)
    col_ok = (kcol >= wstart) & (kcol < wstart + WIN_W)
    dc = jnp.clip(kcol - qc, -(WIN_W - 1), WIN_W - 1) + (WIN_W - 1)
    t = jnp.where(col_ok[None, None], rpb[:, :, dc].astype(F32), NEG)
    blocks = []
    for i in range(R // Q_ROWS):
        r = Q_ROWS * i + jnp.arange(Q_ROWS)[:, None]
        kr = jnp.clip(Q_ROWS * i - WIN_H // 2, 0, R - K_ROWS) + jnp.arange(K_ROWS)[None, :]
        rstart = jnp.clip(r - kh // 2, 0, R - kh)
        row_ok = (kr >= rstart) & (kr < rstart + kh)
        dr = jnp.clip(kr - r + (WIN_H - 1), 0, 2 * WIN_H - 2)
        b = jnp.where(row_ok[None, :, :, None, None], t[:, dr], NEG)
        blocks.append(b.transpose(0, 1, 3, 2, 4).reshape(NA_HEADS, Q_ROWS * GRID_W, K_ROWS * GRID_W))
    return jnp.stack(blocks, axis=1)


def _short_conv_kernel(u_ref, w_ref, b_ref, o_ref):
    u = u_ref[0]
    n = u.shape[0]
    row = lax.broadcasted_iota(jnp.int32, u.shape, 0)
    prev = jnp.where(row == 0, 0.0, pltpu.roll(u, 1, 0))
    nxt = jnp.where(row == n - 1, 0.0, pltpu.roll(u, n - 1, 0))
    w = w_ref[...]
    o_ref[0] = prev * w[0:1] + u * w[1:2] + nxt * w[2:3] + b_ref[...]


def _short_conv(p, blk0, w, b):
    Bx, Lx, _ = p.shape
    N = w.shape[1]
    tc = 512
    return pl.pallas_call(
        _short_conv_kernel,
        grid=(Bx, N // tc),
        in_specs=[pl.BlockSpec((1, Lx, tc), lambda bi, j: (bi, 0, blk0 + j)),
                  pl.BlockSpec((w.shape[0], tc), lambda bi, j: (0, j)),
                  pl.BlockSpec((1, tc), lambda bi, j: (0, j))],
        out_specs=pl.BlockSpec((1, Lx, tc), lambda bi, j: (bi, 0, j)),
        out_shape=jax.ShapeDtypeStruct((Bx, Lx, N), F32),
        compiler_params=_cparams(("parallel", "parallel"), 40),
        name="hyena_short_conv",
    )(p, w, b.reshape(1, N))


def _dft_mats(L):
    n = 2 * L
    f = jnp.arange(L, dtype=jnp.int32)[:, None]
    s = jnp.arange(L, dtype=jnp.int32)[None, :]
    ang = ((f * s) % n).astype(F32) * (2.0 * math.pi / n)
    alt = jnp.where(s % 2 == 0, 1.0, -1.0).astype(F32)
    c = jnp.cos(ang)
    sm = jnp.where(f == 0, alt, -jnp.sin(ang))
    return c.astype(BF16), sm.astype(BF16), sm.T.astype(BF16)


def _filter_features(L):
    pos = jnp.arange(L, dtype=F32)
    t = pos / max(L - 1, 1)
    w = 2.0 * math.pi * pos / L
    f = jnp.linspace(1e-4, HY_BANDS - 1, HY_BANDS, dtype=F32)
    z = jnp.concatenate([t[:, None], jnp.cos(f[None, :] * w[:, None]), -jnp.sin(f[None, :] * w[:, None])], axis=-1)
    return jnp.pad(z, ((0, 0), (0, LANES - z.shape[1])))


def _filter_kernel(z_ref, w1_ref, b1_ref, w2_ref, b2_ref, w3f_ref, w3b_ref, dl_ref, c_ref, s_ref,
                   p_ref, q_ref, p2_ref):
    L = z_ref.shape[0]
    hid = jnp.sin(HY_SIN_FREQ * (_dot3(z_ref[...], w1_ref[...]) + b1_ref[...]))
    hid = jnp.sin(HY_SIN_FREQ * (_dot3(hid, w2_ref[...]) + b2_ref[...]))
    hf = _dot3(hid, w3f_ref[...])
    hb = _dot3(hid, w3b_ref[...])
    row = lax.broadcasted_iota(jnp.int32, hf.shape, 0)
    t = row.astype(F32) / float(max(L - 1, 1))
    dec = jnp.exp(-t * dl_ref[...])
    hf = hf * dec
    hb = jnp.where(row == 0, 0.0, hb * dec)
    inv = 1.0 / (jnp.sum(jnp.abs(hf), axis=0, keepdims=True) + jnp.sum(jnp.abs(hb), axis=0, keepdims=True) + EPS)
    a = (hf + hb) * inv
    b = (hf - hb) * inv
    nyq = jnp.sum(jnp.where(row % 2 == 0, a, -a), axis=0, keepdims=True)
    hr = _bdot(c_ref[...], a.astype(BF16))
    hi = _bdot(s_ref[...], b.astype(BF16))
    alpha = jnp.where(row == 0, 1.0 / (2 * L), 2.0 / (2 * L))
    p_ref[...] = hr * alpha
    q_ref[...] = jnp.where(row == 0, 0.0, hi) * alpha
    p2_ref[...] = jnp.where(row == 0, nyq, hr) * alpha


def _pad2(a, r, c):
    return jnp.pad(a, ((0, r - a.shape[0]), (0, c - a.shape[1])))


def _hyena_filter(L, w1, b1, w2, b2, w3, dft):
    c, s, _ = dft
    n = HY_ORDER * HY_WIDTH
    tc = 256
    z = _filter_features(L)
    w1p = _pad2(w1, LANES, LANES)
    w2p = _pad2(w2, LANES, LANES)
    w3p = _pad2(w3, LANES, 2 * n)
    b1p = _pad2(b1[None, :], 1, LANES)
    b2p = _pad2(b2[None, :], 1, LANES)
    deltas = jnp.abs(jnp.linspace(HY_MIN_DECAY, HY_MAX_DECAY, HY_WIDTH, dtype=F32)).reshape(1, HY_WIDTH)
    full = lambda shape: pl.BlockSpec(shape, lambda j: (0, 0))
    const = lambda shape: pl.BlockSpec(shape, lambda j: (0, 0), pipeline_mode=pl.Buffered(1))
    osd = jax.ShapeDtypeStruct((L, n), F32)
    ospec = pl.BlockSpec((L, tc), lambda j: (0, j))
    return pl.pallas_call(
        _filter_kernel,
        grid=(n // tc,),
        in_specs=[full((L, LANES)), full((LANES, LANES)), full((1, LANES)), full((LANES, LANES)), full((1, LANES)),
                  pl.BlockSpec((LANES, tc), lambda j: (0, j)),
                  pl.BlockSpec((LANES, tc), lambda j: (0, n // tc + j)),
                  pl.BlockSpec((1, tc), lambda j: (0, j % (HY_WIDTH // tc))),
                  const((L, L)), const((L, L))],
        out_specs=[ospec, ospec, ospec],
        out_shape=[osd, osd, osd],
        compiler_params=_cparams(("parallel",), 56),
        name="hyena_filter",
    )(z, w1p, b1p, w2p, b2p, w3p, w3p, deltas, c, s)


def _hyena_order_kernel(z_ref, g_ref, sk_ref, c_ref, s_ref, st_ref, p_ref, q_ref, p2_ref, o_ref, *, nchunk):
    z = z_ref[0]
    zb = z.astype(BF16)
    L = z.shape[0]
    fc = L // nchunk
    conv = None
    for ci in range(nchunk):
        sl = pl.ds(ci * fc, fc)
        zr = _bdot(c_ref[sl, :], zb)
        zi = _bdot(s_ref[sl, :], zb)
        p, q, p2 = p_ref[sl, :], q_ref[sl, :], p2_ref[sl, :]
        yr = (zr * p - zi * q).astype(BF16)
        yi = (zr * q + zi * p2).astype(BF16)
        part = _bdot(c_ref[:, sl], yr) + _bdot(st_ref[:, sl], yi)
        conv = part if conv is None else conv + part
    o_ref[0] = (g_ref[0] * (conv + z * sk_ref[...])).astype(o_ref.dtype)


def _hyena_order(zsrc, zblk, gsrc, gblk, skip, filt, order, dft, out_dtype):
    Bx, Lx, _ = zsrc.shape
    c, s, st = dft
    p, q, p2 = filt
    tc = 256
    nct = HY_WIDTH // tc
    const = lambda shape: pl.BlockSpec(shape, lambda j, b: (0, 0), pipeline_mode=pl.Buffered(1))
    fspec = pl.BlockSpec((Lx, tc), lambda j, b: (0, order * nct + j), pipeline_mode=pl.Buffered(1))
    return pl.pallas_call(
        functools.partial(_hyena_order_kernel, nchunk=max(1, Lx // 512)),
        grid=(nct, Bx),
        in_specs=[pl.BlockSpec((1, Lx, tc), lambda j, b: (b, 0, zblk + j)),
                  pl.BlockSpec((1, Lx, tc), lambda j, b: (b, 0, gblk + j)),
                  pl.BlockSpec((1, tc), lambda j, b: (0, j)),
                  const((Lx, Lx)), const((Lx, Lx)), const((Lx, Lx)),
                  fspec, fspec, fspec],
        out_specs=pl.BlockSpec((1, Lx, tc), lambda j, b: (b, 0, j)),
        out_shape=jax.ShapeDtypeStruct((Bx, Lx, HY_WIDTH), out_dtype),
        compiler_params=_cparams(("parallel", "arbitrary"), 56),
        name="hyena_long_conv",
    )(zsrc, gsrc, skip.reshape(1, HY_WIDTH), c, s, st, p, q, p2)


def _hyena(p, hy_blk512, short_w, short_b, filt, skip, dft):
    u = _short_conv(p, hy_blk512, short_w, short_b)
    nct = HY_WIDTH // 256
    z1 = _hyena_order(u, 0, u, nct, skip[0], filt, 0, dft, F32)
    return _hyena_order(z1, 0, u, 2 * nct, skip[1], filt, 1, dft, BF16)


def _merge_kernel(a_ref, h_ref, ga_ref, gb_ref, x_ref, g1_ref, wa_ref, wb_ref, wo_ref, o_ref):
    y = _sigmoid(ga_ref[0]) * _bdot(a_ref[0], wa_ref[...]) + _sigmoid(gb_ref[0]) * _bdot(h_ref[0], wb_ref[...])
    o_ref[0] = x_ref[0] + g1_ref[0] * _bdot(y.astype(BF16), wo_ref[...])


def _merge(attn, hyena, p, gate_blk, x, g1, wa, wb, wo):
    Bx, Lx, D = x.shape
    tm = min(Lx, 512)
    W = attn.shape[2]
    const = lambda shape: pl.BlockSpec(shape, lambda b, i: (0, 0))
    return pl.pallas_call(
        _merge_kernel,
        grid=(Bx, Lx // tm),
        in_specs=[pl.BlockSpec((1, tm, W), lambda b, i: (b, i, 0)),
                  pl.BlockSpec((1, tm, W), lambda b, i: (b, i, 0)),
                  pl.BlockSpec((1, tm, D), lambda b, i: (b, i, gate_blk)),
                  pl.BlockSpec((1, tm, D), lambda b, i: (b, i, gate_blk + 1)),
                  pl.BlockSpec((1, tm, D), lambda b, i: (b, i, 0)),
                  pl.BlockSpec((1, 1, D), lambda b, i: (b, 0, 0)),
                  const((W, D)), const((W, D)), const((D, D))],
        out_specs=pl.BlockSpec((1, tm, D), lambda b, i: (b, i, 0)),
        out_shape=jax.ShapeDtypeStruct((Bx, Lx, D), F32),
        compiler_params=_cparams(("parallel", "parallel"), 48),
        name="merge_out_proj",
    )(attn, hyena, p, p, x, g1, wa, wb, wo)


def _router_kernel(x_ref, g_ref, sc_ref, sh_ref, wr_ref, br_ref, h_ref, cw_ref):
    h = _rms_mod(x_ref[0], g_ref[...], sc_ref[0], sh_ref[0])
    h_ref[0] = h.astype(BF16)
    logits = _dot3(h, wr_ref[...]) + br_ref[...]
    lane = lax.broadcasted_iota(jnp.int32, logits.shape, 1)
    lane_f = lane.astype(F32)
    big = float(LANES)
    is_g = (lane >= N_EXPERTS) & (lane < N_EXPERTS + N_GROUPS)
    gl = jnp.where(is_g, logits, NEG)
    gmax = gl.max(axis=-1, keepdims=True)
    gp = 1.0 / jnp.where(is_g, jnp.exp(gl - gmax), 0.0).sum(axis=-1, keepdims=True)
    gidx = jnp.where(is_g & (gl == gmax), lane_f - N_EXPERTS, big).min(axis=-1, keepdims=True)
    in_grp = (lane < N_EXPERTS) & ((lane // EXPERTS_PER_GROUP).astype(F32) == gidx)
    el = jnp.where(in_grp, logits, NEG)
    v1 = el.max(axis=-1, keepdims=True)
    i1 = jnp.where(in_grp & (el == v1), lane_f, big).min(axis=-1, keepdims=True)
    rest = in_grp & (lane_f != i1)
    el2 = jnp.where(rest, logits, NEG)
    v2 = el2.max(axis=-1, keepdims=True)
    i2 = jnp.where(rest & (el2 == v2), lane_f, big).min(axis=-1, keepdims=True)
    e2 = jnp.exp(v2 - v1)
    w1 = gp / (1.0 + e2)
    cw_ref[0] = jnp.where(lane_f == i1, w1, jnp.where(lane_f == i2, w1 * e2, 0.0))


def _router(x, g, sc, sh, w_group, b_group, w_router, b_router):
    Bx, Lx, D = x.shape
    tm = min(Lx, 512)
    wr = _pad2(jnp.concatenate([w_router, w_group], axis=1), D, LANES)
    br = _pad2(jnp.concatenate([b_router, b_group])[None, :], 1, LANES)
    return pl.pallas_call(
        _router_kernel,
        grid=(Bx, Lx // tm),
        in_specs=[pl.BlockSpec((1, tm, D), lambda b, i: (b, i, 0)),
                  pl.BlockSpec((1, D), lambda b, i: (0, 0)),
                  pl.BlockSpec((1, 1, D), lambda b, i: (b, 0, 0)),
                  pl.BlockSpec((1, 1, D), lambda b, i: (b, 0, 0)),
                  pl.BlockSpec((D, LANES), lambda b, i: (0, 0)),
                  pl.BlockSpec((1, LANES), lambda b, i: (0, 0))],
        out_specs=[pl.BlockSpec((1, tm, D), lambda b, i: (b, i, 0)),
                   pl.BlockSpec((1, tm, LANES), lambda b, i: (b, i, 0))],
        out_shape=[jax.ShapeDtypeStruct((Bx, Lx, D), BF16), jax.ShapeDtypeStruct((Bx, Lx, LANES), F32)],
        compiler_params=_cparams(("parallel", "parallel"), 40),
        name="moe_router",
    )(x, g.reshape(1, D), sc, sh, wr, br)


def _moe_kernel(h_ref, cw_ref, x_ref, g2_ref, w1_ref, w3_ref, w2_ref, o_ref, acc_ref):
    e = pl.program_id(2)

    @pl.when(e == 0)
    def _():
        acc_ref[...] = jnp.zeros_like(acc_ref)

    h = h_ref[0]
    a = _bdot(h, w1_ref[0])
    b = _bdot(h, w3_ref[0])
    cw = cw_ref[0]
    lane = lax.broadcasted_iota(jnp.int32, cw.shape, 1)
    cwe = jnp.where(lane == e, cw, 0.0).sum(axis=-1, keepdims=True)
    hid = (a * _sigmoid(a)) * b * cwe
    acc_ref[...] += _bdot(hid.astype(BF16), w2_ref[0])

    @pl.when(e == N_EXPERTS - 1)
    def _():
        o_ref[0] = x_ref[0] + g2_ref[0] * acc_ref[...]


def _moe(h, cw, x, g2, w1, w3, w2):
    Bx, Lx, D = x.shape
    tm = min(Lx, 1024)
    F = D_EXPERT
    return pl.pallas_call(
        _moe_kernel,
        grid=(Bx, Lx // tm, N_EXPERTS),
        in_specs=[pl.BlockSpec((1, tm, D), lambda b, i, e: (b, i, 0)),
                  pl.BlockSpec((1, tm, LANES), lambda b, i, e: (b, i, 0)),
                  pl.BlockSpec((1, tm, D), lambda b, i, e: (b, i, 0)),
                  pl.BlockSpec((1, 1, D), lambda b, i, e: (b, 0, 0)),
                  pl.BlockSpec((1, D, F), lambda b, i, e: (e, 0, 0)),
                  pl.BlockSpec((1, D, F), lambda b, i, e: (e, 0, 0)),
                  pl.BlockSpec((1, F, D), lambda b, i, e: (e, 0, 0))],
        out_specs=pl.BlockSpec((1, tm, D), lambda b, i, e: (b, i, 0)),
        out_shape=jax.ShapeDtypeStruct((Bx, Lx, D), F32),
        scratch_shapes=[pltpu.VMEM((tm, D), F32)],
        compiler_params=_cparams(("parallel", "parallel", "arbitrary"), 48),
        name="moe_experts",
    )(h, cw, x, g2, w1, w3, w2)


def _layer(x, xc, mods, modc, last, lw, consts):
    B, L, D = x.shape
    Lc = xc.shape[1]
    sh1, sc1, g1, sh2, sc2, g2 = mods
    sh1c, sc1c, g1c, sh2c, sc2c, g2c = modc
    w_in = lw["w_in"]
    hy_blk = 3 * NA_WIDTH // HY_WIDTH
    gate_blk = (3 * NA_WIDTH + (HY_ORDER + 1) * HY_WIDTH) // D
    v_blk = 2 * NA_WIDTH // LANES

    p = _norm_mod_matmul(x, lw["norm_mix"], sc1, sh1, w_in)
    q_rot, q_plain, k_rot = _qk_prep(p, 0, 1, lw["q_norm"], lw["k_norm"], consts["rope"])
    if last:
        pc = _norm_mod_matmul(xc, lw["norm_mix"], sc1c, sh1c, w_in[:, NA_WIDTH:3 * NA_WIDTH])
        _, kc = _qk_prep(pc, 0, 0, lw["q_norm"], lw["k_norm"], None)
        vc_blk = NA_WIDTH // LANES
    else:
        pc = _norm_mod_matmul(xc, lw["norm_mix"], sc1c, sh1c, w_in)
        qc, kc = _qk_prep(pc, 0, 1, lw["q_norm"], lw["k_norm"], None)
        vc_blk = v_blk
    attn = _nbr_attention(q_rot, q_plain, k_rot, p, v_blk, kc, pc, vc_blk, _nbr_bias(lw["rpb"], L))
    flt = (lw["flt_w1"], lw["flt_b1"], lw["flt_w2"], lw["flt_b2"], lw["flt_w3"])
    filt = _hyena_filter(L, *flt, consts["dft"])
    hyena = _hyena(p, hy_blk, lw["short_w"], lw["short_b"], filt, lw["hy_skip"], consts["dft"])
    x = _merge(attn, hyena, p, gate_blk, x, g1, lw["w_br_a"], lw["w_br_b"], lw["w_out"])
    h2, cw = _router(x, lw["norm_ffn"], sc2, sh2, lw["w_group"], lw["b_group"], lw["w_router"], lw["b_router"])
    x = _moe(h2, cw, x, g2, lw["moe_w1"], lw["moe_w3"], lw["moe_w2"])
    if last:
        return x, xc

    attn_c = _ctx_attention(qc, kc, pc, vc_blk)
    filt_c = _hyena_filter(Lc, *flt, consts["dft_c"])
    hyena_c = _hyena(pc, hy_blk, lw["short_w"], lw["short_b"], filt_c, lw["hy_skip"], consts["dft_c"])
    xc = _merge(attn_c, hyena_c, pc, gate_blk, xc, g1c, lw["w_br_a"], lw["w_br_b"], lw["w_out"])
    h2c, cwc = _router(xc, lw["norm_ffn"], sc2c, sh2c, lw["w_group"], lw["b_group"], lw["w_router"], lw["b_router"])
    xc = _moe(h2c, cwc, xc, g2c, lw["moe_w1"], lw["moe_w3"], lw["moe_w2"])
    return x, xc


def kernel(x, c, ctx, c_ctx, ada_w, ada_b, norm_mix, norm_ffn, w_in, q_norm, k_norm, rpb, short_w, short_b, flt_w1, flt_b1, flt_w2, flt_b2, flt_w3, hy_skip, w_br_a, w_br_b, w_out, w_group, b_group, w_router, b_router, moe_w1, moe_w3, moe_w2):
    B, L, D = x.shape
    Lc = ctx.shape[1]
    depth = ada_w.shape[0]
    consts = {"rope": _rope_tables(L), "dft": _dft_mats(L), "dft_c": _dft_mats(Lc)}
    rows = 8 * ((B + 1 + 7) // 8)
    cs = jnp.pad(jnp.concatenate([c, c_ctx[None, :]], axis=0), ((0, rows - B - 1), (0, 0)))
    xc = ctx
    for i in range(depth):
        mod = _ada(cs, ada_w[i], ada_b[i])
        mods = [m.reshape(B, 1, D) for m in jnp.split(mod[:B], 6, axis=-1)]
        modc = [jnp.broadcast_to(m.reshape(1, 1, D), (B, 1, D)) for m in jnp.split(mod[B], 6, axis=-1)]
        ne = N_EXPERTS
        lw = {
            "norm_mix": norm_mix[i], "norm_ffn": norm_ffn[i], "w_in": w_in[i].astype(BF16),
            "q_norm": q_norm[i], "k_norm": k_norm[i], "rpb": rpb[i],
            "short_w": short_w[i], "short_b": short_b[i],
            "flt_w1": flt_w1[i], "flt_b1": flt_b1[i], "flt_w2": flt_w2[i], "flt_b2": flt_b2[i], "flt_w3": flt_w3[i],
            "hy_skip": hy_skip[i],
            "w_br_a": w_br_a[i].astype(BF16), "w_br_b": w_br_b[i].astype(BF16), "w_out": w_out[i].astype(BF16),
            "w_group": w_group[i], "b_group": b_group[i], "w_router": w_router[i], "b_router": b_router[i],
            "moe_w1": moe_w1[i].reshape(ne, D, D_EXPERT).astype(BF16),
            "moe_w3": moe_w3[i].reshape(ne, D, D_EXPERT).astype(BF16),
            "moe_w2": moe_w2[i].reshape(ne, D_EXPERT, D).astype(BF16),
        }
        x, xc = _layer(x, xc, mods, modc, i == depth - 1, lw, consts)
    return x
```

```python
import functools
import math

import jax
import jax.numpy as jnp
from jax import lax
from jax.experimental import pallas as pl
from jax.experimental.pallas import tpu as pltpu

F32 = jnp.float32
BF16 = jnp.bfloat16

D_MODEL = 1024
GRID_W = 64
NA_HEADS = 8
HEAD_DIM = 64
NA_WIDTH = NA_HEADS * HEAD_DIM
WIN_H = 8
WIN_W = 16
ROPE_THETA = 100.0
HY_WIDTH = 512
HY_ORDER = 2
HY_BANDS = 16
HY_SIN_FREQ = 1.0
HY_MAX_DECAY = math.log(1e-2) / 0.3
HY_MIN_DECAY = math.log(1e-2) / 1.5
N_GROUPS = 4
EXPERTS_PER_GROUP = 8
N_EXPERTS = N_GROUPS * EXPERTS_PER_GROUP
D_EXPERT = 256
EPS = 1e-6
NEG = -1e30

LANES = 128
V7X_VMEM_BYTES = 64 * 1024 * 1024
Q_ROWS = 8
K_ROWS = 16


def _cparams(sem, vmem_mb):
    assert vmem_mb * 1024 * 1024 < V7X_VMEM_BYTES
    return pltpu.CompilerParams(dimension_semantics=sem, vmem_limit_bytes=vmem_mb * 1024 * 1024)


def _bdot(a, b):
    return jnp.dot(a, b, preferred_element_type=F32)


def _split(a):
    hi = a.astype(BF16)
    lo = (a - hi.astype(F32)).astype(BF16)
    return hi, lo


def _dot3(a, b):
    ah, al = _split(a)
    bh, bl = _split(b)
    return _bdot(ah, bh) + _bdot(ah, bl) + _bdot(al, bh)


def _sigmoid(x):
    return 1.0 / (1.0 + jnp.exp(-x))


def _rms_mod(x, g, sc, sh):
    ms = jnp.mean(x * x, axis=-1, keepdims=True)
    return (x * lax.rsqrt(ms + EPS) * g) * (1.0 + sc) + sh


def _ada_kernel(c_ref, w_ref, b_ref, o_ref):
    c = c_ref[...]
    o_ref[...] = _dot3(c * _sigmoid(c), w_ref[...]) + b_ref[...]


def _ada(cs, w, b):
    R, D = cs.shape
    N = w.shape[1]
    tn = 512
    return pl.pallas_call(
        _ada_kernel,
        grid=(N // tn,),
        in_specs=[pl.BlockSpec((R, D), lambda j: (0, 0)),
                  pl.BlockSpec((D, tn), lambda j: (0, j)),
                  pl.BlockSpec((1, tn), lambda j: (0, j))],
        out_specs=pl.BlockSpec((R, tn), lambda j: (0, j)),
        out_shape=jax.ShapeDtypeStruct((R, N), F32),
        compiler_params=_cparams(("parallel",), 32),
        name="ada_mod",
    )(cs, w, b.reshape(1, N))


def _nmm_kernel(x_ref, g_ref, sc_ref, sh_ref, w_ref, o_ref, h_ref):
    @pl.when(pl.program_id(2) == 0)
    def _():
        h_ref[...] = _rms_mod(x_ref[0], g_ref[...], sc_ref[0], sh_ref[0]).astype(BF16)

    o_ref[0] = _bdot(h_ref[...], w_ref[...]).astype(o_ref.dtype)


def _norm_mod_matmul(x, g, sc, sh, w):
    Bx, Lx, D = x.shape
    N = w.shape[1]
    tm = min(Lx, 512)
    tn = 1280 if N % 1280 == 0 else 1024
    return pl.pallas_call(
        _nmm_kernel,
        grid=(Bx, Lx // tm, N // tn),
        in_specs=[pl.BlockSpec((1, tm, D), lambda b, i, j: (b, i, 0)),
                  pl.BlockSpec((1, D), lambda b, i, j: (0, 0)),
                  pl.BlockSpec((1, 1, D), lambda b, i, j: (b, 0, 0)),
                  pl.BlockSpec((1, 1, D), lambda b, i, j: (b, 0, 0)),
                  pl.BlockSpec((D, tn), lambda b, i, j: (0, j))],
        out_specs=pl.BlockSpec((1, tm, tn), lambda b, i, j: (b, i, j)),
        out_shape=jax.ShapeDtypeStruct((Bx, Lx, N), F32),
        scratch_shapes=[pltpu.VMEM((tm, D), BF16)],
        compiler_params=_cparams(("parallel", "parallel", "arbitrary"), 40),
        name="norm_mod_proj",
    )(x, g.reshape(1, D), sc, sh, w)


def _head_norm(x, gn, bd):
    hi, lo = _split(x * x)
    ms = _bdot(hi, bd) + _bdot(lo, bd)
    return x * lax.rsqrt(ms + EPS) * gn


def _rope(x, cos, sin_signed):
    lane = lax.broadcasted_iota(jnp.int32, x.shape, 1)
    quarter = HEAD_DIM // 4
    partner = jnp.where((lane % (2 * quarter)) < quarter,
                        pltpu.roll(x, x.shape[1] - quarter, 1), pltpu.roll(x, quarter, 1))
    return x * cos + partner * sin_signed


def _qk_rope_kernel(q_ref, k_ref, qn_ref, kn_ref, cos_ref, sin_ref, bd_ref, qr_ref, qp_ref, kr_ref):
    bd = bd_ref[...]
    cos = cos_ref[...]
    sin = sin_ref[...]
    q = _head_norm(q_ref[0], qn_ref[...], bd) * (HEAD_DIM ** -0.5)
    k = _head_norm(k_ref[0], kn_ref[...], bd)
    qp_ref[0] = q.astype(BF16)
    qr_ref[0] = _rope(q, cos, sin).astype(BF16)
    kr_ref[0] = _rope(k, cos, sin).astype(BF16)


def _qk_plain_kernel(q_ref, k_ref, qn_ref, kn_ref, bd_ref, qp_ref, kp_ref):
    bd = bd_ref[...]
    qp_ref[0] = (_head_norm(q_ref[0], qn_ref[...], bd) * (HEAD_DIM ** -0.5)).astype(BF16)
    kp_ref[0] = _head_norm(k_ref[0], kn_ref[...], bd).astype(BF16)


def _head_block_diag():
    r = jnp.arange(NA_WIDTH) // HEAD_DIM
    return jnp.where(r[:, None] == r[None, :], 1.0 / HEAD_DIM, 0.0).astype(BF16)


def _qk_prep(p, qcol, kcol, qn, kn, rope_tabs):
    Bx, Lx, _ = p.shape
    W = NA_WIDTH
    tm = min(Lx, 512)
    tok = lambda c: pl.BlockSpec((1, tm, W), lambda b, i: (b, i, c))
    vec = pl.BlockSpec((1, W), lambda b, i: (0, 0))
    mat = pl.BlockSpec((W, W), lambda b, i: (0, 0))
    out = pl.BlockSpec((1, tm, W), lambda b, i: (b, i, 0))
    osd = jax.ShapeDtypeStruct((Bx, Lx, W), BF16)
    qn_t = jnp.tile(qn, NA_HEADS).reshape(1, W)
    kn_t = jnp.tile(kn, NA_HEADS).reshape(1, W)
    bd = _head_block_diag()
    if rope_tabs is None:
        return pl.pallas_call(
            _qk_plain_kernel, grid=(Bx, Lx // tm),
            in_specs=[tok(qcol), tok(kcol), vec, vec, mat],
            out_specs=[out, out], out_shape=[osd, osd],
            compiler_params=_cparams(("parallel", "parallel"), 32), name="qk_norm",
        )(p, p, qn_t, kn_t, bd)
    cos, sin = rope_tabs
    tab = pl.BlockSpec((tm, W), lambda b, i: (i, 0))
    return pl.pallas_call(
        _qk_rope_kernel, grid=(Bx, Lx // tm),
        in_specs=[tok(qcol), tok(kcol), vec, vec, tab, tab, mat],
        out_specs=[out, out, out], out_shape=[osd, osd, osd],
        compiler_params=_cparams(("parallel", "parallel"), 32), name="qk_norm_rope",
    )(p, p, qn_t, kn_t, cos, sin, bd)


def _rope_tables(L):
    quarter = HEAD_DIM // 4
    freqs = ROPE_THETA ** (-jnp.arange(quarter, dtype=F32) / quarter)
    pos = jnp.arange(L)
    rows, cols = (pos // GRID_W).astype(F32), (pos % GRID_W).astype(F32)
    d = jnp.arange(NA_WIDTH) % HEAD_DIM
    p = jnp.where((d < HEAD_DIM // 2)[None, :], rows[:, None], cols[:, None])
    ang = p * freqs[d % quarter][None, :]
    sign = jnp.where((d % (2 * quarter)) < quarter, -1.0, 1.0)[None, :]
    return jnp.cos(ang), jnp.sin(ang) * sign


def _softmax_pv(s_list, v_list):
    m = s_list[0].max(axis=-1, keepdims=True)
    for s in s_list[1:]:
        m = jnp.maximum(m, s.max(axis=-1, keepdims=True))
    l = 0.0
    o = 0.0
    for s, v in zip(s_list, v_list):
        p = jnp.exp(s - m)
        l = l + p.sum(axis=-1, keepdims=True)
        o = o + _bdot(p.astype(BF16), v)
    return o / l


def _qkt(q, k):
    return lax.dot_general(q, k, (((1,), (1,)), ((), ())), preferred_element_type=F32)


def _head_lanes(x, lane, hh):
    return jnp.where((lane // HEAD_DIM) == hh, x, 0.0).astype(BF16)


def _nbr_attn_kernel(q_ref, qp_ref, k_ref, v_ref, kc_ref, vc_ref, bias_ref, o_ref):
    i = pl.program_id(1)
    max_row0 = k_ref.shape[1] // GRID_W - K_ROWS
    k0 = pl.multiple_of(jnp.clip(Q_ROWS * i - WIN_H // 2, 0, max_row0) * GRID_W, 4 * GRID_W)
    nk = K_ROWS * GRID_W
    k = k_ref[0, pl.ds(k0, nk), :]
    v = v_ref[0, pl.ds(k0, nk), :].astype(BF16)
    kc = kc_ref[0]
    vc = vc_ref[0].astype(BF16)
    q = q_ref[0].astype(F32)
    qp = qp_ref[0].astype(F32)
    lane = lax.broadcasted_iota(jnp.int32, q.shape, 1)
    outs = []
    for hh in range(LANES // HEAD_DIM):
        s_w = _qkt(_head_lanes(q, lane, hh), k) + bias_ref[hh, 0]
        s_c = _qkt(_head_lanes(qp, lane, hh), kc)
        outs.append(_softmax_pv([s_w, s_c], [v, vc]))
    o_ref[0] = jnp.where(lane < HEAD_DIM, outs[0], outs[1]).astype(o_ref.dtype)


def _nbr_attention(q_rot, q_plain, k_rot, p, v_blk, kc, pc, vc_blk, bias):
    B, L, _ = q_rot.shape
    Lc = kc.shape[1]
    hp_n = NA_WIDTH // LANES
    nq = Q_ROWS * GRID_W
    ni = L // nq
    heads_per = LANES // HEAD_DIM
    return pl.pallas_call(
        _nbr_attn_kernel,
        grid=(hp_n, ni, B),
        in_specs=[pl.BlockSpec((1, nq, LANES), lambda h, i, b: (b, i, h)),
                  pl.BlockSpec((1, nq, LANES), lambda h, i, b: (b, i, h)),
                  pl.BlockSpec((1, L, LANES), lambda h, i, b: (b, 0, h)),
                  pl.BlockSpec((1, L, LANES), lambda h, i, b: (b, 0, v_blk + h)),
                  pl.BlockSpec((1, Lc, LANES), lambda h, i, b: (b, 0, h)),
                  pl.BlockSpec((1, Lc, LANES), lambda h, i, b: (b, 0, vc_blk + h)),
                  pl.BlockSpec((heads_per, 1, nq, K_ROWS * GRID_W), lambda h, i, b: (h, i, 0, 0))],
        out_specs=pl.BlockSpec((1, nq, LANES), lambda h, i, b: (b, i, h)),
        out_shape=jax.ShapeDtypeStruct((B, L, NA_WIDTH), BF16),
        compiler_params=_cparams(("parallel", "parallel", "arbitrary"), 48),
        name="nbr_attention",
    )(q_rot, q_plain, k_rot, p, kc, pc, bias)


def _ctx_attn_kernel(q_ref, k_ref, v_ref, o_ref):
    q = q_ref[0].astype(F32)
    k = k_ref[0]
    v = v_ref[0].astype(BF16)
    lane = lax.broadcasted_iota(jnp.int32, q.shape, 1)
    outs = []
    for hh in range(LANES // HEAD_DIM):
        outs.append(_softmax_pv([_qkt(_head_lanes(q, lane, hh), k)], [v]))
    o_ref[0] = jnp.where(lane < HEAD_DIM, outs[0], outs[1]).astype(o_ref.dtype)


def _ctx_attention(qc, kc, pc, vc_blk):
    B, Lc, _ = qc.shape
    blk = lambda off: pl.BlockSpec((1, Lc, LANES), lambda h, b: (b, 0, off + h))
    return pl.pallas_call(
        _ctx_attn_kernel,
        grid=(NA_WIDTH // LANES, B),
        in_specs=[blk(0), blk(0), blk(vc_blk)],
        out_specs=blk(0),
        out_shape=jax.ShapeDtypeStruct((B, Lc, NA_WIDTH), BF16),
        compiler_params=_cparams(("parallel", "parallel"), 32),
        name="ctx_attention",
    )(qc, kc, pc)


def _nbr_bias(rpb, L):
    R = L // GRID_W
    kh = min(WIN_H, R)
    qc = jnp.arange(GRID_W)[:, None]
    kcol = jnp.arange(GRID_W)[None, :]
    wstart = jnp.clip(qc - WIN_W // 2, 0, GRID_W - WIN_W)
    col_ok = (kcol >= wstart) & (kcol < wstart + WIN_W)
    dc = jnp.clip(kcol - qc, -(WIN_W - 1), WIN_W - 1) + (WIN_W - 1)
    t = jnp.where(col_ok[None, None], rpb[:, :, dc].astype(F32), NEG)
    blocks = []
    for i in range(R // Q_ROWS):
        r = Q_ROWS * i + jnp.arange(Q_ROWS)[:, None]
        kr = jnp.clip(Q_ROWS * i - WIN_H // 2, 0, R - K_ROWS) + jnp.arange(K_ROWS)[None, :]
        rstart = jnp.clip(r - kh // 2, 0, R - kh)
        row_ok = (kr >= rstart) & (kr < rstart + kh)
        dr = jnp.clip(kr - r + (WIN_H - 1), 0, 2 * WIN_H - 2)
        b = jnp.where(row_ok[None, :, :, None, None], t[:, dr], NEG)
        blocks.append(b.transpose(0, 1, 3, 2, 4).reshape(NA_HEADS, Q_ROWS * GRID_W, K_ROWS * GRID_W))
    return jnp.stack(blocks, axis=1)


def _short_conv_kernel(u_ref, w_ref, b_ref, o_ref):
    u = u_ref[0]
    n = u.shape[0]
    row = lax.broadcasted_iota(jnp.int32, u.shape, 0)
    prev = jnp.where(row == 0, 0.0, pltpu.roll(u, 1, 0))
    nxt = jnp.where(row == n - 1, 0.0, pltpu.roll(u, n - 1, 0))
    w = w_ref[...]
    o_ref[0] = prev * w[0:1] + u * w[1:2] + nxt * w[2:3] + b_ref[...]


def _short_conv(p, blk0, w, b):
    Bx, Lx, _ = p.shape
    N = w.shape[1]
    tc = 512
    return pl.pallas_call(
        _short_conv_kernel,
        grid=(Bx, N // tc),
        in_specs=[pl.BlockSpec((1, Lx, tc), lambda bi, j: (bi, 0, blk0 + j)),
                  pl.BlockSpec((w.shape[0], tc), lambda bi, j: (0, j)),
                  pl.BlockSpec((1, tc), lambda bi, j: (0, j))],
        out_specs=pl.BlockSpec((1, Lx, tc), lambda bi, j: (bi, 0, j)),
        out_shape=jax.ShapeDtypeStruct((Bx, Lx, N), F32),
        compiler_params=_cparams(("parallel", "parallel"), 40),
        name="hyena_short_conv",
    )(p, w, b.reshape(1, N))


def _dft_mats(L):
    n = 2 * L
    f = jnp.arange(L, dtype=jnp.int32)[:, None]
    s = jnp.arange(L, dtype=jnp.int32)[None, :]
    ang = ((f * s) % n).astype(F32) * (2.0 * math.pi / n)
    alt = jnp.where(s % 2 == 0, 1.0, -1.0).astype(F32)
    c = jnp.cos(ang)
    sm = jnp.where(f == 0, alt, -jnp.sin(ang))
    return c.astype(BF16), sm.astype(BF16), sm.T.astype(BF16)


def _filter_features(L):
    pos = jnp.arange(L, dtype=F32)
    t = pos / max(L - 1, 1)
    w = 2.0 * math.pi * pos / L
    f = jnp.linspace(1e-4, HY_BANDS - 1, HY_BANDS, dtype=F32)
    z = jnp.concatenate([t[:, None], jnp.cos(f[None, :] * w[:, None]), -jnp.sin(f[None, :] * w[:, None])], axis=-1)
    return jnp.pad(z, ((0, 0), (0, LANES - z.shape[1])))


def _filter_kernel(z_ref, w1_ref, b1_ref, w2_ref, b2_ref, w3f_ref, w3b_ref, dl_ref, c_ref, s_ref,
                   p_ref, q_ref, p2_ref):
    L = z_ref.shape[0]
    hid = jnp.sin(HY_SIN_FREQ * (_dot3(z_ref[...], w1_ref[...]) + b1_ref[...]))
    hid = jnp.sin(HY_SIN_FREQ * (_dot3(hid, w2_ref[...]) + b2_ref[...]))
    hf = _dot3(hid, w3f_ref[...])
    hb = _dot3(hid, w3b_ref[...])
    row = lax.broadcasted_iota(jnp.int32, hf.shape, 0)
    t = row.astype(F32) / float(max(L - 1, 1))
    dec = jnp.exp(-t * dl_ref[...])
    hf = hf * dec
    hb = jnp.where(row == 0, 0.0, hb * dec)
    inv = 1.0 / (jnp.sum(jnp.abs(hf), axis=0, keepdims=True) + jnp.sum(jnp.abs(hb), axis=0, keepdims=True) + EPS)
    a = (hf + hb) * inv
    b = (hf - hb) * inv
    nyq = jnp.sum(jnp.where(row % 2 == 0, a, -a), axis=0, keepdims=True)
    hr = _bdot(c_ref[...], a.astype(BF16))
    hi = _bdot(s_ref[...], b.astype(BF16))
    alpha = jnp.where(row == 0, 1.0 / (2 * L), 2.0 / (2 * L))
    p_ref[...] = hr * alpha
    q_ref[...] = jnp.where(row == 0, 0.0, hi) * alpha
    p2_ref[...] = jnp.where(row == 0, nyq, hr) * alpha


def _pad2(a, r, c):
    return jnp.pad(a, ((0, r - a.shape[0]), (0, c - a.shape[1])))


def _hyena_filter(L, w1, b1, w2, b2, w3, dft):
    c, s, _ = dft
    n = HY_ORDER * HY_WIDTH
    tc = 256
    z = _filter_features(L)
    w1p = _pad2(w1, LANES, LANES)
    w2p = _pad2(w2, LANES, LANES)
    w3p = _pad2(w3, LANES, 2 * n)
    b1p = _pad2(b1[None, :], 1, LANES)
    b2p = _pad2(b2[None, :], 1, LANES)
    deltas = jnp.abs(jnp.linspace(HY_MIN_DECAY, HY_MAX_DECAY, HY_WIDTH, dtype=F32)).reshape(1, HY_WIDTH)
    full = lambda shape: pl.BlockSpec(shape, lambda j: (0, 0))
    const = lambda shape: pl.BlockSpec(shape, lambda j: (0, 0), pipeline_mode=pl.Buffered(1))
    osd = jax.ShapeDtypeStruct((L, n), F32)
    ospec = pl.BlockSpec((L, tc), lambda j: (0, j))
    return pl.pallas_call(
        _filter_kernel,
        grid=(n // tc,),
        in_specs=[full((L, LANES)), full((LANES, LANES)), full((1, LANES)), full((LANES, LANES)), full((1, LANES)),
                  pl.BlockSpec((LANES, tc), lambda j: (0, j)),
                  pl.BlockSpec((LANES, tc), lambda j: (0, n // tc + j)),
                  pl.BlockSpec((1, tc), lambda j: (0, j % (HY_WIDTH // tc))),
                  const((L, L)), const((L, L))],
        out_specs=[ospec, ospec, ospec],
        out_shape=[osd, osd, osd],
        compiler_params=_cparams(("parallel",), 56),
        name="hyena_filter",
    )(z, w1p, b1p, w2p, b2p, w3p, w3p, deltas, c, s)


def _hyena_order_kernel(z_ref, g_ref, sk_ref, c_ref, s_ref, st_ref, p_ref, q_ref, p2_ref, o_ref, *, nchunk):
    z = z_ref[0]
    zb = z.astype(BF16)
    L = z.shape[0]
    fc = L // nchunk
    conv = None
    for ci in range(nchunk):
        sl = pl.ds(ci * fc, fc)
        zr = _bdot(c_ref[sl, :], zb)
        zi = _bdot(s_ref[sl, :], zb)
        p, q, p2 = p_ref[sl, :], q_ref[sl, :], p2_ref[sl, :]
        yr = (zr * p - zi * q).astype(BF16)
        yi = (zr * q + zi * p2).astype(BF16)
        part = _bdot(c_ref[:, sl], yr) + _bdot(st_ref[:, sl], yi)
        conv = part if conv is None else conv + part
    o_ref[0] = (g_ref[0] * (conv + z * sk_ref[...])).astype(o_ref.dtype)


def _hyena_order(zsrc, zblk, gsrc, gblk, skip, filt, order, dft, out_dtype):
    Bx, Lx, _ = zsrc.shape
    c, s, st = dft
    p, q, p2 = filt
    tc = 256
    nct = HY_WIDTH // tc
    const = lambda shape: pl.BlockSpec(shape, lambda j, b: (0, 0), pipeline_mode=pl.Buffered(1))
    fspec = pl.BlockSpec((Lx, tc), lambda j, b: (0, order * nct + j), pipeline_mode=pl.Buffered(1))
    return pl.pallas_call(
        functools.partial(_hyena_order_kernel, nchunk=max(1, Lx // 512)),
        grid=(nct, Bx),
        in_specs=[pl.BlockSpec((1, Lx, tc), lambda j, b: (b, 0, zblk + j)),
                  pl.BlockSpec((1, Lx, tc), lambda j, b: (b, 0, gblk + j)),
                  pl.BlockSpec((1, tc), lambda j, b: (0, j)),
                  const((Lx, Lx)), const((Lx, Lx)), const((Lx, Lx)),
                  fspec, fspec, fspec],
        out_specs=pl.BlockSpec((1, Lx, tc), lambda j, b: (b, 0, j)),
        out_shape=jax.ShapeDtypeStruct((Bx, Lx, HY_WIDTH), out_dtype),
        compiler_params=_cparams(("parallel", "arbitrary"), 56),
        name="hyena_long_conv",
    )(zsrc, gsrc, skip.reshape(1, HY_WIDTH), c, s, st, p, q, p2)


def _hyena(p, hy_blk512, short_w, short_b, filt, skip, dft):
    u = _short_conv(p, hy_blk512, short_w, short_b)
    nct = HY_WIDTH // 256
    z1 = _hyena_order(u, 0, u, nct, skip[0], filt, 0, dft, F32)
    return _hyena_order(z1, 0, u, 2 * nct, skip[1], filt, 1, dft, BF16)


def _merge_kernel(a_ref, h_ref, ga_ref, gb_ref, x_ref, g1_ref, wa_ref, wb_ref, wo_ref, o_ref):
    y = _sigmoid(ga_ref[0]) * _bdot(a_ref[0], wa_ref[...]) + _sigmoid(gb_ref[0]) * _bdot(h_ref[0], wb_ref[...])
    o_ref[0] = x_ref[0] + g1_ref[0] * _bdot(y.astype(BF16), wo_ref[...])


def _merge(attn, hyena, p, gate_blk, x, g1, wa, wb, wo):
    Bx, Lx, D = x.shape
    tm = min(Lx, 512)
    W = attn.shape[2]
    const = lambda shape: pl.BlockSpec(shape, lambda b, i: (0, 0))
    return pl.pallas_call(
        _merge_kernel,
        grid=(Bx, Lx // tm),
        in_specs=[pl.BlockSpec((1, tm, W), lambda b, i: (b, i, 0)),
                  pl.BlockSpec((1, tm, W), lambda b, i: (b, i, 0)),
                  pl.BlockSpec((1, tm, D), lambda b, i: (b, i, gate_blk)),
                  pl.BlockSpec((1, tm, D), lambda b, i: (b, i, gate_blk + 1)),
                  pl.BlockSpec((1, tm, D), lambda b, i: (b, i, 0)),
                  pl.BlockSpec((1, 1, D), lambda b, i: (b, 0, 0)),
                  const((W, D)), const((W, D)), const((D, D))],
        out_specs=pl.BlockSpec((1, tm, D), lambda b, i: (b, i, 0)),
        out_shape=jax.ShapeDtypeStruct((Bx, Lx, D), F32),
        compiler_params=_cparams(("parallel", "parallel"), 48),
        name="merge_out_proj",
    )(attn, hyena, p, p, x, g1, wa, wb, wo)


MOE_TILE = 1024
MOE_CHUNK = 256
GROUP_LANE = N_EXPERTS
RANK_LANE = N_EXPERTS + 1


def _router_kernel(x_ref, g_ref, sc_ref, sh_ref, wr_ref, br_ref, ht_ref, cw_ref, cwt_ref):
    h = _rms_mod(x_ref[0], g_ref[...], sc_ref[0], sh_ref[0])
    ht_ref[0] = h.T.astype(BF16)
    logits = _dot3(h, wr_ref[...]) + br_ref[...]
    lane = lax.broadcasted_iota(jnp.int32, logits.shape, 1)
    lane_f = lane.astype(F32)
    big = float(LANES)
    is_g = (lane >= N_EXPERTS) & (lane < N_EXPERTS + N_GROUPS)
    gl = jnp.where(is_g, logits, NEG)
    gmax = gl.max(axis=-1, keepdims=True)
    gp = 1.0 / jnp.where(is_g, jnp.exp(gl - gmax), 0.0).sum(axis=-1, keepdims=True)
    gidx = jnp.where(is_g & (gl == gmax), lane_f - N_EXPERTS, big).min(axis=-1, keepdims=True)
    in_grp = (lane < N_EXPERTS) & ((lane // EXPERTS_PER_GROUP).astype(F32) == gidx)
    el = jnp.where(in_grp, logits, NEG)
    v1 = el.max(axis=-1, keepdims=True)
    i1 = jnp.where(in_grp & (el == v1), lane_f, big).min(axis=-1, keepdims=True)
    rest = in_grp & (lane_f != i1)
    el2 = jnp.where(rest, logits, NEG)
    v2 = el2.max(axis=-1, keepdims=True)
    i2 = jnp.where(rest & (el2 == v2), lane_f, big).min(axis=-1, keepdims=True)
    e2 = jnp.exp(v2 - v1)
    w1 = gp / (1.0 + e2)
    cw = jnp.where(lane_f == i1, w1, jnp.where(lane_f == i2, w1 * e2, 0.0))
    tm = h.shape[0]
    onehot = jnp.where((lane_f == gidx) & (lane < N_GROUPS), 1.0, 0.0)
    tri = lax.broadcasted_iota(jnp.int32, (tm, tm), 1) <= lax.broadcasted_iota(jnp.int32, (tm, tm), 0)
    cum = _bdot(jnp.where(tri, 1.0, 0.0).astype(BF16), onehot.astype(BF16))
    rank = (onehot * cum).sum(axis=-1, keepdims=True) - 1.0
    rec = jnp.where(lane == GROUP_LANE, gidx, jnp.where(lane == RANK_LANE, rank, cw))
    cw_ref[0] = rec
    cwt_ref[0] = rec.T


def _router(x, g, sc, sh, w_group, b_group, w_router, b_router):
    Bx, Lx, D = x.shape
    tm = MOE_TILE
    wr = _pad2(jnp.concatenate([w_router, w_group], axis=1), D, LANES)
    br = _pad2(jnp.concatenate([b_router, b_group])[None, :], 1, LANES)
    return pl.pallas_call(
        _router_kernel,
        grid=(Bx, Lx // tm),
        in_specs=[pl.BlockSpec((1, tm, D), lambda b, i: (b, i, 0)),
                  pl.BlockSpec((1, D), lambda b, i: (0, 0)),
                  pl.BlockSpec((1, 1, D), lambda b, i: (b, 0, 0)),
                  pl.BlockSpec((1, 1, D), lambda b, i: (b, 0, 0)),
                  pl.BlockSpec((D, LANES), lambda b, i: (0, 0)),
                  pl.BlockSpec((1, LANES), lambda b, i: (0, 0))],
        out_specs=[pl.BlockSpec((1, D, tm), lambda b, i: (b, 0, i)),
                   pl.BlockSpec((1, tm, LANES), lambda b, i: (b, i, 0)),
                   pl.BlockSpec((1, LANES, tm), lambda b, i: (b, 0, i))],
        out_shape=[jax.ShapeDtypeStruct((Bx, D, Lx), BF16), jax.ShapeDtypeStruct((Bx, Lx, LANES), F32),
                   jax.ShapeDtypeStruct((Bx, LANES, Lx), F32)],
        compiler_params=_cparams(("parallel", "parallel"), 48),
        name="moe_router",
    )(x, g.reshape(1, D), sc, sh, wr, br)


def _moe_kernel(cnt_ref, ht_ref, cw_ref, cwt_ref, x_ref, g2_ref, w1t_ref, w3t_ref, w2t_ref, o_ref, acc_ref):
    b, i, g = pl.program_id(0), pl.program_id(1), pl.program_id(2)

    @pl.when(g == 0)
    def _():
        acc_ref[...] = jnp.zeros_like(acc_ref)

    n = cnt_ref[(b * pl.num_programs(1) + i) * N_GROUPS + g]
    gf = g.astype(F32)
    tm = acc_ref.shape[1]
    c = MOE_CHUNK
    sel_col = jnp.where(cw_ref[0, :, GROUP_LANE:GROUP_LANE + 1] == gf, cw_ref[0, :, RANK_LANE:RANK_LANE + 1], -1.0)
    sel_row = jnp.where(cwt_ref[0, GROUP_LANE:GROUP_LANE + 1, :] == gf, cwt_ref[0, RANK_LANE:RANK_LANE + 1, :], -1.0)
    e0 = pl.multiple_of(g * EXPERTS_PER_GROUP, EXPERTS_PER_GROUP)
    cw_hi, cw_lo = _split(cwt_ref[0, pl.ds(e0, EXPERTS_PER_GROUP), :])
    lane_r = lax.broadcasted_iota(jnp.int32, (tm, c), 1).astype(F32)
    sub_r = lax.broadcasted_iota(jnp.int32, (c, tm), 0).astype(F32)

    def chunk(k, carry):
        r0 = (k * c).astype(F32)
        pkt = jnp.where(sel_col - r0 == lane_r, 1.0, 0.0).astype(BF16)
        pk = jnp.where(sel_row - r0 == sub_r, 1.0, 0.0).astype(BF16)
        xst = _bdot(ht_ref[0], pkt).astype(BF16)
        cws = _bdot(cw_hi, pkt) + _bdot(cw_lo, pkt)
        at = _bdot(w1t_ref[0], xst)
        bt = _bdot(w3t_ref[0], xst)
        out = None
        for e in range(EXPERTS_PER_GROUP):
            rows = slice(e * D_EXPERT, (e + 1) * D_EXPERT)
            a = at[rows]
            hid = (a * _sigmoid(a)) * bt[rows] * cws[e:e + 1]
            part = _bdot(w2t_ref[0, :, rows], hid.astype(BF16))
            out = part if out is None else out + part
        acc_ref[...] += _bdot(out.astype(BF16), pk)
        return carry

    lax.fori_loop(0, (n + c - 1) // c, chunk, 0)

    @pl.when(g == N_GROUPS - 1)
    def _():
        o_ref[0] = x_ref[0] + g2_ref[0] * acc_ref[...].T


def _moe(ht, cw, cwt, x, g2, w1t, w3t, w2t):
    Bx, Lx, D = x.shape
    tm = MOE_TILE
    nt = Lx // tm
    ef = EXPERTS_PER_GROUP * D_EXPERT
    gid = cw[..., GROUP_LANE].reshape(Bx, nt, tm, 1)
    counts = jnp.sum(gid == jnp.arange(N_GROUPS, dtype=F32), axis=2).astype(jnp.int32).reshape(-1)
    once = pl.Buffered(1)
    grid_spec = pltpu.PrefetchScalarGridSpec(
        num_scalar_prefetch=1,
        grid=(Bx, nt, N_GROUPS),
        in_specs=[pl.BlockSpec((1, D, tm), lambda b, i, g, cnt: (b, 0, i), pipeline_mode=once),
                  pl.BlockSpec((1, tm, LANES), lambda b, i, g, cnt: (b, i, 0)),
                  pl.BlockSpec((1, LANES, tm), lambda b, i, g, cnt: (b, 0, i)),
                  pl.BlockSpec((1, tm, D), lambda b, i, g, cnt: (b, i, 0), pipeline_mode=once),
                  pl.BlockSpec((1, 1, D), lambda b, i, g, cnt: (b, 0, 0)),
                  pl.BlockSpec((1, ef, D), lambda b, i, g, cnt: (g, 0, 0)),
                  pl.BlockSpec((1, ef, D), lambda b, i, g, cnt: (g, 0, 0)),
                  pl.BlockSpec((1, D, ef), lambda b, i, g, cnt: (g, 0, 0))],
        out_specs=pl.BlockSpec((1, tm, D), lambda b, i, g, cnt: (b, i, 0)),
        scratch_shapes=[pltpu.VMEM((D, tm), F32)])
    return pl.pallas_call(
        _moe_kernel,
        grid_spec=grid_spec,
        out_shape=jax.ShapeDtypeStruct((Bx, Lx, D), F32),
        compiler_params=_cparams(("parallel", "parallel", "arbitrary"), 56),
        name="moe_experts",
    )(counts, ht, cw, cwt, x, g2, w1t, w3t, w2t)


def _layer(x, xc, mods, modc, last, lw, consts):
    B, L, D = x.shape
    Lc = xc.shape[1]
    sh1, sc1, g1, sh2, sc2, g2 = mods
    sh1c, sc1c, g1c, sh2c, sc2c, g2c = modc
    w_in = lw["w_in"]
    hy_blk = 3 * NA_WIDTH // HY_WIDTH
    gate_blk = (3 * NA_WIDTH + (HY_ORDER + 1) * HY_WIDTH) // D
    v_blk = 2 * NA_WIDTH // LANES

    p = _norm_mod_matmul(x, lw["norm_mix"], sc1, sh1, w_in)
    q_rot, q_plain, k_rot = _qk_prep(p, 0, 1, lw["q_norm"], lw["k_norm"], consts["rope"])
    if last:
        pc = _norm_mod_matmul(xc, lw["norm_mix"], sc1c, sh1c, w_in[:, NA_WIDTH:3 * NA_WIDTH])
        _, kc = _qk_prep(pc, 0, 0, lw["q_norm"], lw["k_norm"], None)
        vc_blk = NA_WIDTH // LANES
    else:
        pc = _norm_mod_matmul(xc, lw["norm_mix"], sc1c, sh1c, w_in)
        qc, kc = _qk_prep(pc, 0, 1, lw["q_norm"], lw["k_norm"], None)
        vc_blk = v_blk
    attn = _nbr_attention(q_rot, q_plain, k_rot, p, v_blk, kc, pc, vc_blk, _nbr_bias(lw["rpb"], L))
    flt = (lw["flt_w1"], lw["flt_b1"], lw["flt_w2"], lw["flt_b2"], lw["flt_w3"])
    filt = _hyena_filter(L, *flt, consts["dft"])
    hyena = _hyena(p, hy_blk, lw["short_w"], lw["short_b"], filt, lw["hy_skip"], consts["dft"])
    x = _merge(attn, hyena, p, gate_blk, x, g1, lw["w_br_a"], lw["w_br_b"], lw["w_out"])
    rw = (lw["w_group"], lw["b_group"], lw["w_router"], lw["b_router"])
    ew = (lw["moe_w1t"], lw["moe_w3t"], lw["moe_w2t"])
    x = _moe(*_router(x, lw["norm_ffn"], sc2, sh2, *rw), x, g2, *ew)
    if last:
        return x, xc

    attn_c = _ctx_attention(qc, kc, pc, vc_blk)
    filt_c = _hyena_filter(Lc, *flt, consts["dft_c"])
    hyena_c = _hyena(pc, hy_blk, lw["short_w"], lw["short_b"], filt_c, lw["hy_skip"], consts["dft_c"])
    xc = _merge(attn_c, hyena_c, pc, gate_blk, xc, g1c, lw["w_br_a"], lw["w_br_b"], lw["w_out"])
    xf = xc.reshape(1, B * Lc, D)
    xf = _moe(*_router(xf, lw["norm_ffn"], sc2c[:1], sh2c[:1], *rw), xf, g2c[:1], *ew)
    return x, xf.reshape(B, Lc, D)


def kernel(x, c, ctx, c_ctx, ada_w, ada_b, norm_mix, norm_ffn, w_in, q_norm, k_norm, rpb, short_w, short_b, flt_w1, flt_b1, flt_w2, flt_b2, flt_w3, hy_skip, w_br_a, w_br_b, w_out, w_group, b_group, w_router, b_router, moe_w1, moe_w3, moe_w2):
    B, L, D = x.shape
    Lc = ctx.shape[1]
    depth = ada_w.shape[0]
    consts = {"rope": _rope_tables(L), "dft": _dft_mats(L), "dft_c": _dft_mats(Lc)}
    rows = 8 * ((B + 1 + 7) // 8)
    cs = jnp.pad(jnp.concatenate([c, c_ctx[None, :]], axis=0), ((0, rows - B - 1), (0, 0)))
    xc = ctx
    for i in range(depth):
        mod = _ada(cs, ada_w[i], ada_b[i])
        mods = [m.reshape(B, 1, D) for m in jnp.split(mod[:B], 6, axis=-1)]
        modc = [jnp.broadcast_to(m.reshape(1, 1, D), (B, 1, D)) for m in jnp.split(mod[B], 6, axis=-1)]
        ef = EXPERTS_PER_GROUP * D_EXPERT
        lw = {
            "norm_mix": norm_mix[i], "norm_ffn": norm_ffn[i], "w_in": w_in[i].astype(BF16),
            "q_norm": q_norm[i], "k_norm": k_norm[i], "rpb": rpb[i],
            "short_w": short_w[i], "short_b": short_b[i],
            "flt_w1": flt_w1[i], "flt_b1": flt_b1[i], "flt_w2": flt_w2[i], "flt_b2": flt_b2[i], "flt_w3": flt_w3[i],
            "hy_skip": hy_skip[i],
            "w_br_a": w_br_a[i].astype(BF16), "w_br_b": w_br_b[i].astype(BF16), "w_out": w_out[i].astype(BF16),
            "w_group": w_group[i], "b_group": b_group[i], "w_router": w_router[i], "b_router": b_router[i],
            "moe_w1t": moe_w1[i].astype(BF16).transpose(0, 1, 3, 2).reshape(N_GROUPS, ef, D),
            "moe_w3t": moe_w3[i].astype(BF16).transpose(0, 1, 3, 2).reshape(N_GROUPS, ef, D),
            "moe_w2t": moe_w2[i].astype(BF16).transpose(0, 3, 1, 2).reshape(N_GROUPS, D, ef),
        }
        x, xc = _layer(x, xc, mods, modc, i == depth - 1, lw, consts)
    return x
```

```python
import functools
import math

import jax
import jax.numpy as jnp
from jax import lax
from jax.experimental import pallas as pl
from jax.experimental.pallas import tpu as pltpu

F32 = jnp.float32
BF16 = jnp.bfloat16

D_MODEL = 1024
GRID_W = 64
NA_HEADS = 8
HEAD_DIM = 64
NA_WIDTH = NA_HEADS * HEAD_DIM
WIN_H = 8
WIN_W = 16
ROPE_THETA = 100.0
HY_WIDTH = 512
HY_ORDER = 2
HY_BANDS = 16
HY_SIN_FREQ = 1.0
HY_MAX_DECAY = math.log(1e-2) / 0.3
HY_MIN_DECAY = math.log(1e-2) / 1.5
N_GROUPS = 4
EXPERTS_PER_GROUP = 8
N_EXPERTS = N_GROUPS * EXPERTS_PER_GROUP
D_EXPERT = 256
EPS = 1e-6
NEG = -1e30

LANES = 128
V7X_VMEM_BYTES = 64 * 1024 * 1024
Q_ROWS = 8
K_ROWS = 16


def _cparams(sem, vmem_mb):
    assert vmem_mb * 1024 * 1024 < V7X_VMEM_BYTES
    return pltpu.CompilerParams(dimension_semantics=sem, vmem_limit_bytes=vmem_mb * 1024 * 1024)


def _bdot(a, b):
    return jnp.dot(a, b, preferred_element_type=F32)


def _split(a):
    hi = a.astype(BF16)
    lo = (a - hi.astype(F32)).astype(BF16)
    return hi, lo


def _dot3(a, b):
    ah, al = _split(a)
    bh, bl = _split(b)
    return _bdot(ah, bh) + _bdot(ah, bl) + _bdot(al, bh)


def _sigmoid(x):
    return 1.0 / (1.0 + jnp.exp(-x))


def _rms_mod(x, g, sc, sh):
    ms = jnp.mean(x * x, axis=-1, keepdims=True)
    return (x * lax.rsqrt(ms + EPS) * g) * (1.0 + sc) + sh


def _ada_kernel(c_ref, w_ref, b_ref, o_ref):
    c = c_ref[...]
    o_ref[...] = _dot3(c * _sigmoid(c), w_ref[...]) + b_ref[...]


def _ada(cs, w, b):
    R, D = cs.shape
    N = w.shape[1]
    tn = 512
    return pl.pallas_call(
        _ada_kernel,
        grid=(N // tn,),
        in_specs=[pl.BlockSpec((R, D), lambda j: (0, 0)),
                  pl.BlockSpec((D, tn), lambda j: (0, j)),
                  pl.BlockSpec((1, tn), lambda j: (0, j))],
        out_specs=pl.BlockSpec((R, tn), lambda j: (0, j)),
        out_shape=jax.ShapeDtypeStruct((R, N), F32),
        compiler_params=_cparams(("parallel",), 32),
        name="ada_mod",
    )(cs, w, b.reshape(1, N))


def _nmm_kernel(x_ref, g_ref, sc_ref, sh_ref, w_ref, o_ref, h_ref):
    @pl.when(pl.program_id(2) == 0)
    def _():
        h_ref[...] = _rms_mod(x_ref[0], g_ref[...], sc_ref[0], sh_ref[0]).astype(BF16)

    o_ref[0] = _bdot(h_ref[...], w_ref[...]).astype(o_ref.dtype)


def _norm_mod_matmul(x, g, sc, sh, w):
    Bx, Lx, D = x.shape
    N = w.shape[1]
    tm = min(Lx, 1024)
    tn = 1280 if N % 1280 == 0 else 1024
    return pl.pallas_call(
        _nmm_kernel,
        grid=(Bx, Lx // tm, N // tn),
        in_specs=[pl.BlockSpec((1, tm, D), lambda b, i, j: (b, i, 0)),
                  pl.BlockSpec((1, D), lambda b, i, j: (0, 0)),
                  pl.BlockSpec((1, 1, D), lambda b, i, j: (b, 0, 0)),
                  pl.BlockSpec((1, 1, D), lambda b, i, j: (b, 0, 0)),
                  pl.BlockSpec((D, tn), lambda b, i, j: (0, j))],
        out_specs=pl.BlockSpec((1, tm, tn), lambda b, i, j: (b, i, j)),
        out_shape=jax.ShapeDtypeStruct((Bx, Lx, N), BF16),
        scratch_shapes=[pltpu.VMEM((tm, D), BF16)],
        compiler_params=_cparams(("parallel", "parallel", "arbitrary"), 40),
        name="norm_mod_proj",
    )(x, g.reshape(1, D), sc, sh, w)


def _head_norm(x, gn, bd):
    hi, lo = _split(x * x)
    ms = _bdot(hi, bd) + _bdot(lo, bd)
    return x * lax.rsqrt(ms + EPS) * gn


def _rope(x, cos, sin_signed):
    lane = lax.broadcasted_iota(jnp.int32, x.shape, 1)
    quarter = HEAD_DIM // 4
    partner = jnp.where((lane % (2 * quarter)) < quarter,
                        pltpu.roll(x, x.shape[1] - quarter, 1), pltpu.roll(x, quarter, 1))
    return x * cos + partner * sin_signed


def _qk_rope_kernel(q_ref, k_ref, qn_ref, kn_ref, cos_ref, sin_ref, bd_ref, qr_ref, qp_ref, kr_ref):
    bd = bd_ref[...]
    cos = cos_ref[...]
    sin = sin_ref[...]
    q = _head_norm(q_ref[0].astype(F32), qn_ref[...], bd) * (HEAD_DIM ** -0.5)
    k = _head_norm(k_ref[0].astype(F32), kn_ref[...], bd)
    qp_ref[0] = q.astype(BF16)
    qr_ref[0] = _rope(q, cos, sin).astype(BF16)
    kr_ref[0] = _rope(k, cos, sin).astype(BF16)


def _qk_plain_kernel(q_ref, k_ref, qn_ref, kn_ref, bd_ref, qp_ref, kp_ref):
    bd = bd_ref[...]
    qp_ref[0] = (_head_norm(q_ref[0].astype(F32), qn_ref[...], bd) * (HEAD_DIM ** -0.5)).astype(BF16)
    kp_ref[0] = _head_norm(k_ref[0].astype(F32), kn_ref[...], bd).astype(BF16)


def _head_block_diag():
    r = jnp.arange(NA_WIDTH) // HEAD_DIM
    return jnp.where(r[:, None] == r[None, :], 1.0 / HEAD_DIM, 0.0).astype(BF16)


def _qk_prep(p, qcol, kcol, qn, kn, rope_tabs):
    Bx, Lx, _ = p.shape
    W = NA_WIDTH
    tm = min(Lx, 512)
    tok = lambda c: pl.BlockSpec((1, tm, W), lambda b, i: (b, i, c))
    vec = pl.BlockSpec((1, W), lambda b, i: (0, 0))
    mat = pl.BlockSpec((W, W), lambda b, i: (0, 0))
    out = pl.BlockSpec((1, tm, W), lambda b, i: (b, i, 0))
    osd = jax.ShapeDtypeStruct((Bx, Lx, W), BF16)
    qn_t = jnp.tile(qn, NA_HEADS).reshape(1, W)
    kn_t = jnp.tile(kn, NA_HEADS).reshape(1, W)
    bd = _head_block_diag()
    if rope_tabs is None:
        return pl.pallas_call(
            _qk_plain_kernel, grid=(Bx, Lx // tm),
            in_specs=[tok(qcol), tok(kcol), vec, vec, mat],
            out_specs=[out, out], out_shape=[osd, osd],
            compiler_params=_cparams(("parallel", "parallel"), 32), name="qk_norm",
        )(p, p, qn_t, kn_t, bd)
    cos, sin = rope_tabs
    tab = pl.BlockSpec((tm, W), lambda b, i: (i, 0))
    return pl.pallas_call(
        _qk_rope_kernel, grid=(Bx, Lx // tm),
        in_specs=[tok(qcol), tok(kcol), vec, vec, tab, tab, mat],
        out_specs=[out, out, out], out_shape=[osd, osd, osd],
        compiler_params=_cparams(("parallel", "parallel"), 32), name="qk_norm_rope",
    )(p, p, qn_t, kn_t, cos, sin, bd)


def _rope_tables(L):
    quarter = HEAD_DIM // 4
    freqs = ROPE_THETA ** (-jnp.arange(quarter, dtype=F32) / quarter)
    pos = jnp.arange(L)
    rows, cols = (pos // GRID_W).astype(F32), (pos % GRID_W).astype(F32)
    d = jnp.arange(NA_WIDTH) % HEAD_DIM
    p = jnp.where((d < HEAD_DIM // 2)[None, :], rows[:, None], cols[:, None])
    ang = p * freqs[d % quarter][None, :]
    sign = jnp.where((d % (2 * quarter)) < quarter, -1.0, 1.0)[None, :]
    return jnp.cos(ang), jnp.sin(ang) * sign


def _softmax_pv(s_list, v_list):
    m = s_list[0].max(axis=-1, keepdims=True)
    for s in s_list[1:]:
        m = jnp.maximum(m, s.max(axis=-1, keepdims=True))
    l = 0.0
    o = 0.0
    for s, v in zip(s_list, v_list):
        p = jnp.exp(s - m)
        l = l + p.sum(axis=-1, keepdims=True)
        o = o + _bdot(p.astype(BF16), v)
    return o / l


def _qkt(q, k):
    return lax.dot_general(q, k, (((1,), (1,)), ((), ())), preferred_element_type=F32)


def _head_lanes(x, lane, hh):
    return jnp.where((lane // HEAD_DIM) == hh, x, 0.0).astype(BF16)


def _nbr_attn_kernel(q_ref, qp_ref, k_ref, v_ref, kc_ref, vc_ref, bias_ref, o_ref):
    i = pl.program_id(1)
    max_row0 = k_ref.shape[1] // GRID_W - K_ROWS
    k0 = pl.multiple_of(jnp.clip(Q_ROWS * i - WIN_H // 2, 0, max_row0) * GRID_W, 4 * GRID_W)
    nk = K_ROWS * GRID_W
    k = k_ref[0, pl.ds(k0, nk), :]
    v = v_ref[0, pl.ds(k0, nk), :].astype(BF16)
    kc = kc_ref[0]
    vc = vc_ref[0].astype(BF16)
    q = q_ref[0].astype(F32)
    qp = qp_ref[0].astype(F32)
    lane = lax.broadcasted_iota(jnp.int32, q.shape, 1)
    outs = []
    for hh in range(LANES // HEAD_DIM):
        s_w = _qkt(_head_lanes(q, lane, hh), k) + bias_ref[hh, 0]
        s_c = _qkt(_head_lanes(qp, lane, hh), kc)
        outs.append(_softmax_pv([s_w, s_c], [v, vc]))
    o_ref[0] = jnp.where(lane < HEAD_DIM, outs[0], outs[1]).astype(o_ref.dtype)


def _nbr_attention(q_rot, q_plain, k_rot, p, v_blk, kc, pc, vc_blk, bias):
    B, L, _ = q_rot.shape
    Lc = kc.shape[1]
    hp_n = NA_WIDTH // LANES
    nq = Q_ROWS * GRID_W
    ni = L // nq
    heads_per = LANES // HEAD_DIM
    return pl.pallas_call(
        _nbr_attn_kernel,
        grid=(hp_n, ni, B),
        in_specs=[pl.BlockSpec((1, nq, LANES), lambda h, i, b: (b, i, h)),
                  pl.BlockSpec((1, nq, LANES), lambda h, i, b: (b, i, h)),
                  pl.BlockSpec((1, L, LANES), lambda h, i, b: (b, 0, h)),
                  pl.BlockSpec((1, L, LANES), lambda h, i, b: (b, 0, v_blk + h)),
                  pl.BlockSpec((1, Lc, LANES), lambda h, i, b: (b, 0, h)),
                  pl.BlockSpec((1, Lc, LANES), lambda h, i, b: (b, 0, vc_blk + h)),
                  pl.BlockSpec((heads_per, 1, nq, K_ROWS * GRID_W), lambda h, i, b: (h, i, 0, 0))],
        out_specs=pl.BlockSpec((1, nq, LANES), lambda h, i, b: (b, i, h)),
        out_shape=jax.ShapeDtypeStruct((B, L, NA_WIDTH), BF16),
        compiler_params=_cparams(("parallel", "parallel", "arbitrary"), 48),
        name="nbr_attention",
    )(q_rot, q_plain, k_rot, p, kc, pc, bias)


def _ctx_attn_kernel(q_ref, k_ref, v_ref, o_ref):
    q = q_ref[0].astype(F32)
    k = k_ref[0]
    v = v_ref[0].astype(BF16)
    lane = lax.broadcasted_iota(jnp.int32, q.shape, 1)
    outs = []
    for hh in range(LANES // HEAD_DIM):
        outs.append(_softmax_pv([_qkt(_head_lanes(q, lane, hh), k)], [v]))
    o_ref[0] = jnp.where(lane < HEAD_DIM, outs[0], outs[1]).astype(o_ref.dtype)


def _ctx_attention(qc, kc, pc, vc_blk):
    B, Lc, _ = qc.shape
    blk = lambda off: pl.BlockSpec((1, Lc, LANES), lambda h, b: (b, 0, off + h))
    return pl.pallas_call(
        _ctx_attn_kernel,
        grid=(NA_WIDTH // LANES, B),
        in_specs=[blk(0), blk(0), blk(vc_blk)],
        out_specs=blk(0),
        out_shape=jax.ShapeDtypeStruct((B, Lc, NA_WIDTH), BF16),
        compiler_params=_cparams(("parallel", "parallel"), 32),
        name="ctx_attention",
    )(qc, kc, pc)


def _nbr_bias(rpb, L):
    R = L // GRID_W
    kh = min(WIN_H, R)
    qc = jnp.arange(GRID_W)[:, None]
    kcol = jnp.arange(GRID_W)[None, :]
    wstart = jnp.clip(qc - WIN_W // 2, 0, GRID_W - WIN_W)
    col_ok = (kcol >= wstart) & (kcol < wstart + WIN_W)
    dc = jnp.clip(kcol - qc, -(WIN_W - 1), WIN_W - 1) + (WIN_W - 1)
    t = jnp.where(col_ok[None, None], rpb[:, :, dc].astype(F32), NEG)
    t_cat = t.transpose(0, 2, 1, 3).reshape(NA_HEADS, GRID_W, (2 * WIN_H - 1) * GRID_W)
    rows = []
    for r in range(R):
        k0 = min(max(Q_ROWS * (r // Q_ROWS) - WIN_H // 2, 0), R - K_ROWS)
        rstart = min(max(r - kh // 2, 0), R - kh)
        a_lo = rstart - r + (WIN_H - 1)
        win = t_cat[:, :, a_lo * GRID_W:(a_lo + kh) * GRID_W]
        left = (rstart - k0) * GRID_W
        right = (K_ROWS - kh) * GRID_W - left
        rows.append(jnp.pad(win, ((0, 0), (0, 0), (left, right)), constant_values=NEG))
    return jnp.stack(rows, axis=1).reshape(NA_HEADS, R // Q_ROWS, Q_ROWS * GRID_W, K_ROWS * GRID_W)


def _short_conv(u, w, b):
    n = u.shape[0]
    row = lax.broadcasted_iota(jnp.int32, u.shape, 0)
    prev = jnp.where(row == 0, 0.0, pltpu.roll(u, 1, 0))
    nxt = jnp.where(row == n - 1, 0.0, pltpu.roll(u, n - 1, 0))
    return prev * w[0:1] + u * w[1:2] + nxt * w[2:3] + b


def _dft_mats(L):
    n = 2 * L
    f = jnp.arange(L, dtype=jnp.int32)[:, None]
    s = jnp.arange(L, dtype=jnp.int32)[None, :]
    ang = ((f * s) % n).astype(F32) * (2.0 * math.pi / n)
    alt = jnp.where(s % 2 == 0, 1.0, -1.0).astype(F32)
    c = jnp.cos(ang)
    sm = jnp.where(f == 0, alt, -jnp.sin(ang))
    return c.astype(BF16), sm.astype(BF16), sm.T.astype(BF16)


def _filter_features(L):
    pos = jnp.arange(L, dtype=F32)
    t = pos / max(L - 1, 1)
    w = 2.0 * math.pi * pos / L
    f = jnp.linspace(1e-4, HY_BANDS - 1, HY_BANDS, dtype=F32)
    z = jnp.concatenate([t[:, None], jnp.cos(f[None, :] * w[:, None]), -jnp.sin(f[None, :] * w[:, None])], axis=-1)
    return jnp.pad(z, ((0, 0), (0, LANES - z.shape[1])))


def _filter_kernel(z_ref, w1_ref, b1_ref, w2_ref, b2_ref, w3f_ref, w3b_ref, dl_ref, c_ref, s_ref,
                   p_ref, q_ref, p2_ref):
    L = z_ref.shape[0]
    hid = jnp.sin(HY_SIN_FREQ * (_dot3(z_ref[...], w1_ref[...]) + b1_ref[...]))
    hid = jnp.sin(HY_SIN_FREQ * (_dot3(hid, w2_ref[...]) + b2_ref[...]))
    hf = _dot3(hid, w3f_ref[...])
    hb = _dot3(hid, w3b_ref[...])
    row = lax.broadcasted_iota(jnp.int32, hf.shape, 0)
    t = row.astype(F32) / float(max(L - 1, 1))
    dec = jnp.exp(-t * dl_ref[...])
    hf = hf * dec
    hb = jnp.where(row == 0, 0.0, hb * dec)
    inv = 1.0 / (jnp.sum(jnp.abs(hf), axis=0, keepdims=True) + jnp.sum(jnp.abs(hb), axis=0, keepdims=True) + EPS)
    a = (hf + hb) * inv
    b = (hf - hb) * inv
    nyq = jnp.sum(jnp.where(row % 2 == 0, a, -a), axis=0, keepdims=True)
    hr = _bdot(c_ref[...], a.astype(BF16))
    hi = _bdot(s_ref[...], b.astype(BF16))
    alpha = jnp.where(row == 0, 1.0 / (2 * L), 2.0 / (2 * L))
    p_ref[...] = hr * alpha
    q_ref[...] = jnp.where(row == 0, 0.0, hi) * alpha
    p2_ref[...] = jnp.where(row == 0, nyq, hr) * alpha


def _pad2(a, r, c):
    return jnp.pad(a, ((0, r - a.shape[0]), (0, c - a.shape[1])))


def _hyena_filter(L, w1, b1, w2, b2, w3, dft):
    c, s, _ = dft
    n = HY_ORDER * HY_WIDTH
    tc = 256
    z = _filter_features(L)
    w1p = _pad2(w1, LANES, LANES)
    w2p = _pad2(w2, LANES, LANES)
    w3p = _pad2(w3, LANES, 2 * n)
    b1p = _pad2(b1[None, :], 1, LANES)
    b2p = _pad2(b2[None, :], 1, LANES)
    deltas = jnp.abs(jnp.linspace(HY_MIN_DECAY, HY_MAX_DECAY, HY_WIDTH, dtype=F32)).reshape(1, HY_WIDTH)
    full = lambda shape: pl.BlockSpec(shape, lambda j: (0, 0))
    const = lambda shape: pl.BlockSpec(shape, lambda j: (0, 0), pipeline_mode=pl.Buffered(1))
    osd = jax.ShapeDtypeStruct((L, n), F32)
    ospec = pl.BlockSpec((L, tc), lambda j: (0, j))
    return pl.pallas_call(
        _filter_kernel,
        grid=(n // tc,),
        in_specs=[full((L, LANES)), full((LANES, LANES)), full((1, LANES)), full((LANES, LANES)), full((1, LANES)),
                  pl.BlockSpec((LANES, tc), lambda j: (0, j)),
                  pl.BlockSpec((LANES, tc), lambda j: (0, n // tc + j)),
                  pl.BlockSpec((1, tc), lambda j: (0, j % (HY_WIDTH // tc))),
                  const((L, L)), const((L, L))],
        out_specs=[ospec, ospec, ospec],
        out_shape=[osd, osd, osd],
        compiler_params=_cparams(("parallel",), 56),
        name="hyena_filter",
    )(z, w1p, b1p, w2p, b2p, w3p, w3p, deltas, c, s)


def _hyena_order_kernel(z_ref, g_ref, swz_ref, sbz_ref, swg_ref, sbg_ref, sk_ref, c_ref, s_ref, st_ref,
                        p_ref, q_ref, p2_ref, o_ref, *, nchunk, conv_z):
    z = z_ref[0].astype(F32)
    if conv_z:
        z = _short_conv(z, swz_ref[...], sbz_ref[...])
    gate = _short_conv(g_ref[0].astype(F32), swg_ref[...], sbg_ref[...])
    zb = z.astype(BF16)
    L = z.shape[0]
    fc = L // nchunk
    conv = None
    for ci in range(nchunk):
        sl = pl.ds(ci * fc, fc)
        zr = _bdot(c_ref[sl, :], zb)
        zi = _bdot(s_ref[sl, :], zb)
        p, q, p2 = p_ref[sl, :], q_ref[sl, :], p2_ref[sl, :]
        yr = (zr * p - zi * q).astype(BF16)
        yi = (zr * q + zi * p2).astype(BF16)
        part = _bdot(c_ref[:, sl], yr) + _bdot(st_ref[:, sl], yi)
        conv = part if conv is None else conv + part
    o_ref[0] = (gate * (conv + z * sk_ref[...])).astype(o_ref.dtype)


def _hyena_order(zsrc, zblk, conv_z, p, hy_blk, part_z, part_g, short_w, short_b, skip, filt, order, dft, out_dtype):
    Bx, Lx, _ = zsrc.shape
    c, s, st = dft
    fp, fq, fp2 = filt
    tc = 256
    nct = HY_WIDTH // tc
    taps = short_w.shape[0]
    const = lambda shape: pl.BlockSpec(shape, lambda j, b: (0, 0), pipeline_mode=pl.Buffered(1))
    fspec = pl.BlockSpec((Lx, tc), lambda j, b: (0, order * nct + j), pipeline_mode=pl.Buffered(1))
    wspec = lambda part: pl.BlockSpec((taps, tc), lambda j, b: (0, part * nct + j))
    bspec = lambda part: pl.BlockSpec((1, tc), lambda j, b: (0, part * nct + j))
    return pl.pallas_call(
        functools.partial(_hyena_order_kernel, nchunk=max(1, Lx // 512), conv_z=conv_z),
        grid=(nct, Bx),
        in_specs=[pl.BlockSpec((1, Lx, tc), lambda j, b: (b, 0, zblk + j)),
                  pl.BlockSpec((1, Lx, tc), lambda j, b: (b, 0, hy_blk + part_g * nct + j)),
                  wspec(part_z), bspec(part_z), wspec(part_g), bspec(part_g),
                  pl.BlockSpec((1, tc), lambda j, b: (0, j)),
                  const((Lx, Lx)), const((Lx, Lx)), const((Lx, Lx)),
                  fspec, fspec, fspec],
        out_specs=pl.BlockSpec((1, Lx, tc), lambda j, b: (b, 0, j)),
        out_shape=jax.ShapeDtypeStruct((Bx, Lx, HY_WIDTH), out_dtype),
        compiler_params=_cparams(("parallel", "arbitrary"), 56),
        name="hyena_long_conv",
    )(zsrc, p, short_w, short_b.reshape(1, -1), short_w, short_b.reshape(1, -1), skip.reshape(1, HY_WIDTH),
      c, s, st, fp, fq, fp2)


def _hyena(p, hy_blk, short_w, short_b, filt, skip, dft):
    z1 = _hyena_order(p, hy_blk, True, p, hy_blk, 0, 1, short_w, short_b, skip[0], filt, 0, dft, F32)
    return _hyena_order(z1, 0, False, p, hy_blk, 0, 2, short_w, short_b, skip[1], filt, 1, dft, BF16)


def _merge_kernel(a_ref, h_ref, ga_ref, gb_ref, x_ref, g1_ref, wa_ref, wb_ref, wo_ref, o_ref):
    ga = _sigmoid(ga_ref[0].astype(F32))
    gb = _sigmoid(gb_ref[0].astype(F32))
    y = ga * _bdot(a_ref[0], wa_ref[...]) + gb * _bdot(h_ref[0], wb_ref[...])
    o_ref[0] = x_ref[0] + g1_ref[0] * _bdot(y.astype(BF16), wo_ref[...])


def _merge(attn, hyena, p, gate_blk, x, g1, wa, wb, wo):
    Bx, Lx, D = x.shape
    tm = min(Lx, 512)
    W = attn.shape[2]
    const = lambda shape: pl.BlockSpec(shape, lambda b, i: (0, 0))
    return pl.pallas_call(
        _merge_kernel,
        grid=(Bx, Lx // tm),
        in_specs=[pl.BlockSpec((1, tm, W), lambda b, i: (b, i, 0)),
                  pl.BlockSpec((1, tm, W), lambda b, i: (b, i, 0)),
                  pl.BlockSpec((1, tm, D), lambda b, i: (b, i, gate_blk)),
                  pl.BlockSpec((1, tm, D), lambda b, i: (b, i, gate_blk + 1)),
                  pl.BlockSpec((1, tm, D), lambda b, i: (b, i, 0)),
                  pl.BlockSpec((1, 1, D), lambda b, i: (b, 0, 0)),
                  const((W, D)), const((W, D)), const((D, D))],
        out_specs=pl.BlockSpec((1, tm, D), lambda b, i: (b, i, 0)),
        out_shape=jax.ShapeDtypeStruct((Bx, Lx, D), F32),
        compiler_params=_cparams(("parallel", "parallel"), 48),
        name="merge_out_proj",
    )(attn, hyena, p, p, x, g1, wa, wb, wo)


MOE_TILE = 1024
MOE_CHUNK = 256
GROUP_LANE = N_EXPERTS
RANK_LANE = N_EXPERTS + 1


def _router_kernel(x_ref, g_ref, sc_ref, sh_ref, wr_ref, br_ref, ht_ref, cw_ref, cwt_ref):
    h = _rms_mod(x_ref[0], g_ref[...], sc_ref[0], sh_ref[0])
    ht_ref[0] = h.T.astype(BF16)
    logits = _dot3(h, wr_ref[...]) + br_ref[...]
    lane = lax.broadcasted_iota(jnp.int32, logits.shape, 1)
    lane_f = lane.astype(F32)
    big = float(LANES)
    is_g = (lane >= N_EXPERTS) & (lane < N_EXPERTS + N_GROUPS)
    gl = jnp.where(is_g, logits, NEG)
    gmax = gl.max(axis=-1, keepdims=True)
    gp = 1.0 / jnp.where(is_g, jnp.exp(gl - gmax), 0.0).sum(axis=-1, keepdims=True)
    gidx = jnp.where(is_g & (gl == gmax), lane_f - N_EXPERTS, big).min(axis=-1, keepdims=True)
    in_grp = (lane < N_EXPERTS) & ((lane // EXPERTS_PER_GROUP).astype(F32) == gidx)
    el = jnp.where(in_grp, logits, NEG)
    v1 = el.max(axis=-1, keepdims=True)
    i1 = jnp.where(in_grp & (el == v1), lane_f, big).min(axis=-1, keepdims=True)
    rest = in_grp & (lane_f != i1)
    el2 = jnp.where(rest, logits, NEG)
    v2 = el2.max(axis=-1, keepdims=True)
    i2 = jnp.where(rest & (el2 == v2), lane_f, big).min(axis=-1, keepdims=True)
    e2 = jnp.exp(v2 - v1)
    w1 = gp / (1.0 + e2)
    cw = jnp.where(lane_f == i1, w1, jnp.where(lane_f == i2, w1 * e2, 0.0))
    tm = h.shape[0]
    onehot = jnp.where((lane_f == gidx) & (lane < N_GROUPS), 1.0, 0.0)
    tri = lax.broadcasted_iota(jnp.int32, (tm, tm), 1) <= lax.broadcasted_iota(jnp.int32, (tm, tm), 0)
    cum = _bdot(jnp.where(tri, 1.0, 0.0).astype(BF16), onehot.astype(BF16))
    rank = (onehot * cum).sum(axis=-1, keepdims=True) - 1.0
    rec = jnp.where(lane == GROUP_LANE, gidx, jnp.where(lane == RANK_LANE, rank, cw))
    cw_ref[0] = rec
    cwt_ref[0] = rec.T


def _router(x, g, sc, sh, w_group, b_group, w_router, b_router):
    Bx, Lx, D = x.shape
    tm = MOE_TILE
    wr = _pad2(jnp.concatenate([w_router, w_group], axis=1), D, LANES)
    br = _pad2(jnp.concatenate([b_router, b_group])[None, :], 1, LANES)
    return pl.pallas_call(
        _router_kernel,
        grid=(Bx, Lx // tm),
        in_specs=[pl.BlockSpec((1, tm, D), lambda b, i: (b, i, 0)),
                  pl.BlockSpec((1, D), lambda b, i: (0, 0)),
                  pl.BlockSpec((1, 1, D), lambda b, i: (b, 0, 0)),
                  pl.BlockSpec((1, 1, D), lambda b, i: (b, 0, 0)),
                  pl.BlockSpec((D, LANES), lambda b, i: (0, 0)),
                  pl.BlockSpec((1, LANES), lambda b, i: (0, 0))],
        out_specs=[pl.BlockSpec((1, D, tm), lambda b, i: (b, 0, i)),
                   pl.BlockSpec((1, tm, LANES), lambda b, i: (b, i, 0)),
                   pl.BlockSpec((1, LANES, tm), lambda b, i: (b, 0, i))],
        out_shape=[jax.ShapeDtypeStruct((Bx, D, Lx), BF16), jax.ShapeDtypeStruct((Bx, Lx, LANES), F32),
                   jax.ShapeDtypeStruct((Bx, LANES, Lx), F32)],
        compiler_params=_cparams(("parallel", "parallel"), 48),
        name="moe_router",
    )(x, g.reshape(1, D), sc, sh, wr, br)


def _moe_kernel(cnt_ref, ht_ref, cw_ref, cwt_ref, x_ref, g2_ref, w1t_ref, w3t_ref, w2t_ref, o_ref, acc_ref):
    b, i, g = pl.program_id(0), pl.program_id(1), pl.program_id(2)

    @pl.when(g == 0)
    def _():
        acc_ref[...] = jnp.zeros_like(acc_ref)

    n = cnt_ref[(b * pl.num_programs(1) + i) * N_GROUPS + g]
    gf = g.astype(F32)
    tm = acc_ref.shape[1]
    c = MOE_CHUNK
    sel_col = jnp.where(cw_ref[0, :, GROUP_LANE:GROUP_LANE + 1] == gf, cw_ref[0, :, RANK_LANE:RANK_LANE + 1], -1.0)
    sel_row = jnp.where(cwt_ref[0, GROUP_LANE:GROUP_LANE + 1, :] == gf, cwt_ref[0, RANK_LANE:RANK_LANE + 1, :], -1.0)
    e0 = pl.multiple_of(g * EXPERTS_PER_GROUP, EXPERTS_PER_GROUP)
    cw_hi, cw_lo = _split(cwt_ref[0, pl.ds(e0, EXPERTS_PER_GROUP), :])
    lane_r = lax.broadcasted_iota(jnp.int32, (tm, c), 1).astype(F32)
    sub_r = lax.broadcasted_iota(jnp.int32, (c, tm), 0).astype(F32)

    def chunk(k, carry):
        r0 = (k * c).astype(F32)
        pkt = jnp.where(sel_col - r0 == lane_r, 1.0, 0.0).astype(BF16)
        pk = jnp.where(sel_row - r0 == sub_r, 1.0, 0.0).astype(BF16)
        xst = _bdot(ht_ref[0], pkt).astype(BF16)
        cws = _bdot(cw_hi, pkt) + _bdot(cw_lo, pkt)
        at = _bdot(w1t_ref[0], xst)
        bt = _bdot(w3t_ref[0], xst)
        out = None
        for e in range(EXPERTS_PER_GROUP):
            rows = slice(e * D_EXPERT, (e + 1) * D_EXPERT)
            a = at[rows]
            hid = (a * _sigmoid(a)) * bt[rows] * cws[e:e + 1]
            part = _bdot(w2t_ref[0, :, rows], hid.astype(BF16))
            out = part if out is None else out + part
        acc_ref[...] += _bdot(out.astype(BF16), pk)
        return carry

    lax.fori_loop(0, (n + c - 1) // c, chunk, 0)

    @pl.when(g == N_GROUPS - 1)
    def _():
        o_ref[0] = x_ref[0] + g2_ref[0] * acc_ref[...].T


def _moe(ht, cw, cwt, x, g2, w1t, w3t, w2t):
    Bx, Lx, D = x.shape
    tm = MOE_TILE
    nt = Lx // tm
    ef = EXPERTS_PER_GROUP * D_EXPERT
    gid = cw[..., GROUP_LANE].reshape(Bx, nt, tm, 1)
    counts = jnp.sum(gid == jnp.arange(N_GROUPS, dtype=F32), axis=2).astype(jnp.int32).reshape(-1)
    once = pl.Buffered(1)
    grid_spec = pltpu.PrefetchScalarGridSpec(
        num_scalar_prefetch=1,
        grid=(Bx, nt, N_GROUPS),
        in_specs=[pl.BlockSpec((1, D, tm), lambda b, i, g, cnt: (b, 0, i), pipeline_mode=once),
                  pl.BlockSpec((1, tm, LANES), lambda b, i, g, cnt: (b, i, 0)),
                  pl.BlockSpec((1, LANES, tm), lambda b, i, g, cnt: (b, 0, i)),
                  pl.BlockSpec((1, tm, D), lambda b, i, g, cnt: (b, i, 0), pipeline_mode=once),
                  pl.BlockSpec((1, 1, D), lambda b, i, g, cnt: (b, 0, 0)),
                  pl.BlockSpec((1, ef, D), lambda b, i, g, cnt: (g, 0, 0)),
                  pl.BlockSpec((1, ef, D), lambda b, i, g, cnt: (g, 0, 0)),
                  pl.BlockSpec((1, D, ef), lambda b, i, g, cnt: (g, 0, 0))],
        out_specs=pl.BlockSpec((1, tm, D), lambda b, i, g, cnt: (b, i, 0)),
        scratch_shapes=[pltpu.VMEM((D, tm), F32)])
    return pl.pallas_call(
        _moe_kernel,
        grid_spec=grid_spec,
        out_shape=jax.ShapeDtypeStruct((Bx, Lx, D), F32),
        compiler_params=_cparams(("parallel", "parallel", "arbitrary"), 56),
        name="moe_experts",
    )(counts, ht, cw, cwt, x, g2, w1t, w3t, w2t)


def _layer(x, xc, mods, modc, last, lw, consts):
    B, L, D = x.shape
    Lc = xc.shape[1]
    sh1, sc1, g1, sh2, sc2, g2 = mods
    sh1c, sc1c, g1c, sh2c, sc2c, g2c = modc
    w_in = lw["w_in"]
    hy_blk = 3 * NA_WIDTH // 256
    gate_blk = (3 * NA_WIDTH + (HY_ORDER + 1) * HY_WIDTH) // D
    v_blk = 2 * NA_WIDTH // LANES

    p = _norm_mod_matmul(x, lw["norm_mix"], sc1, sh1, w_in)
    q_rot, q_plain, k_rot = _qk_prep(p, 0, 1, lw["q_norm"], lw["k_norm"], consts["rope"])
    if last:
        pc = _norm_mod_matmul(xc, lw["norm_mix"], sc1c, sh1c, w_in[:, NA_WIDTH:3 * NA_WIDTH])
        _, kc = _qk_prep(pc, 0, 0, lw["q_norm"], lw["k_norm"], None)
        vc_blk = NA_WIDTH // LANES
    else:
        pc = _norm_mod_matmul(xc, lw["norm_mix"], sc1c, sh1c, w_in)
        qc, kc = _qk_prep(pc, 0, 1, lw["q_norm"], lw["k_norm"], None)
        vc_blk = v_blk
    attn = _nbr_attention(q_rot, q_plain, k_rot, p, v_blk, kc, pc, vc_blk, _nbr_bias(lw["rpb"], L))
    flt = (lw["flt_w1"], lw["flt_b1"], lw["flt_w2"], lw["flt_b2"], lw["flt_w3"])
    filt = _hyena_filter(L, *flt, consts["dft"])
    hyena = _hyena(p, hy_blk, lw["short_w"], lw["short_b"], filt, lw["hy_skip"], consts["dft"])
    x = _merge(attn, hyena, p, gate_blk, x, g1, lw["w_br_a"], lw["w_br_b"], lw["w_out"])
    rw = (lw["w_group"], lw["b_group"], lw["w_router"], lw["b_router"])
    ew = (lw["moe_w1t"], lw["moe_w3t"], lw["moe_w2t"])
    x = _moe(*_router(x, lw["norm_ffn"], sc2, sh2, *rw), x, g2, *ew)
    if last:
        return x, xc

    attn_c = _ctx_attention(qc, kc, pc, vc_blk)
    filt_c = _hyena_filter(Lc, *flt, consts["dft_c"])
    hyena_c = _hyena(pc, hy_blk, lw["short_w"], lw["short_b"], filt_c, lw["hy_skip"], consts["dft_c"])
    xc = _merge(attn_c, hyena_c, pc, gate_blk, xc, g1c, lw["w_br_a"], lw["w_br_b"], lw["w_out"])
    xf = xc.reshape(1, B * Lc, D)
    xf = _moe(*_router(xf, lw["norm_ffn"], sc2c[:1], sh2c[:1], *rw), xf, g2c[:1], *ew)
    return x, xf.reshape(B, Lc, D)


def kernel(x, c, ctx, c_ctx, ada_w, ada_b, norm_mix, norm_ffn, w_in, q_norm, k_norm, rpb, short_w, short_b, flt_w1, flt_b1, flt_w2, flt_b2, flt_w3, hy_skip, w_br_a, w_br_b, w_out, w_group, b_group, w_router, b_router, moe_w1, moe_w3, moe_w2):
    B, L, D = x.shape
    Lc = ctx.shape[1]
    depth = ada_w.shape[0]
    consts = {"rope": _rope_tables(L), "dft": _dft_mats(L), "dft_c": _dft_mats(Lc)}
    rows = 8 * ((B + 1 + 7) // 8)
    cs = jnp.pad(jnp.concatenate([c, c_ctx[None, :]], axis=0), ((0, rows - B - 1), (0, 0)))
    xc = ctx
    for i in range(depth):
        mod = _ada(cs, ada_w[i], ada_b[i])
        mods = [m.reshape(B, 1, D) for m in jnp.split(mod[:B], 6, axis=-1)]
        modc = [jnp.broadcast_to(m.reshape(1, 1, D), (B, 1, D)) for m in jnp.split(mod[B], 6, axis=-1)]
        ef = EXPERTS_PER_GROUP * D_EXPERT
        lw = {
            "norm_mix": norm_mix[i], "norm_ffn": norm_ffn[i], "w_in": w_in[i].astype(BF16),
            "q_norm": q_norm[i], "k_norm": k_norm[i], "rpb": rpb[i],
            "short_w": short_w[i], "short_b": short_b[i],
            "flt_w1": flt_w1[i], "flt_b1": flt_b1[i], "flt_w2": flt_w2[i], "flt_b2": flt_b2[i], "flt_w3": flt_w3[i],
            "hy_skip": hy_skip[i],
            "w_br_a": w_br_a[i].astype(BF16), "w_br_b": w_br_b[i].astype(BF16), "w_out": w_out[i].astype(BF16),
            "w_group": w_group[i], "b_group": b_group[i], "w_router": w_router[i], "b_router": b_router[i],
            "moe_w1t": moe_w1[i].astype(BF16).transpose(0, 1, 3, 2).reshape(N_GROUPS, ef, D),
            "moe_w3t": moe_w3[i].astype(BF16).transpose(0, 1, 3, 2).reshape(N_GROUPS, ef, D),
            "moe_w2t": moe_w2[i].astype(BF16).transpose(0, 3, 1, 2).reshape(N_GROUPS, D, ef),
        }
        x, xc = _layer(x, xc, mods, modc, i == depth - 1, lw, consts)
    return x
```

```python
import functools
import math

import jax
import jax.numpy as jnp
from jax import lax
from jax.experimental import pallas as pl
from jax.experimental.pallas import tpu as pltpu

F32 = jnp.float32
BF16 = jnp.bfloat16

D_MODEL = 1024
GRID_W = 64
NA_HEADS = 8
HEAD_DIM = 64
NA_WIDTH = NA_HEADS * HEAD_DIM
WIN_H = 8
WIN_W = 16
ROPE_THETA = 100.0
HY_WIDTH = 512
HY_ORDER = 2
HY_BANDS = 16
HY_SIN_FREQ = 1.0
HY_MAX_DECAY = math.log(1e-2) / 0.3
HY_MIN_DECAY = math.log(1e-2) / 1.5
N_GROUPS = 4
EXPERTS_PER_GROUP = 8
N_EXPERTS = N_GROUPS * EXPERTS_PER_GROUP
D_EXPERT = 256
EPS = 1e-6
NEG = -1e30

LANES = 128
V7X_VMEM_BYTES = 64 * 1024 * 1024
ATT_W = 256
Q_ROWS = 4
K_ROWS = Q_ROWS + WIN_H


def _cparams(sem, vmem_mb):
    assert vmem_mb * 1024 * 1024 < V7X_VMEM_BYTES
    return pltpu.CompilerParams(dimension_semantics=sem, vmem_limit_bytes=vmem_mb * 1024 * 1024)


def _bdot(a, b):
    return jnp.dot(a, b, preferred_element_type=F32)


def _split(a):
    hi = a.astype(BF16)
    lo = (a - hi.astype(F32)).astype(BF16)
    return hi, lo


def _dot3(a, b):
    ah, al = _split(a)
    bh, bl = _split(b)
    return _bdot(ah, bh) + _bdot(ah, bl) + _bdot(al, bh)


def _sigmoid(x):
    return 1.0 / (1.0 + jnp.exp(-x))


def _rms_mod(x, g, sc, sh):
    ms = jnp.mean(x * x, axis=-1, keepdims=True)
    return (x * lax.rsqrt(ms + EPS) * g) * (1.0 + sc) + sh


def _ada_kernel(c_ref, w_ref, b_ref, o_ref):
    c = c_ref[...]
    o_ref[...] = _dot3(c * _sigmoid(c), w_ref[...]) + b_ref[...]


def _ada(cs, w, b):
    R, D = cs.shape
    N = w.shape[1]
    tn = 512
    return pl.pallas_call(
        _ada_kernel,
        grid=(N // tn,),
        in_specs=[pl.BlockSpec((R, D), lambda j: (0, 0)),
                  pl.BlockSpec((D, tn), lambda j: (0, j)),
                  pl.BlockSpec((1, tn), lambda j: (0, j))],
        out_specs=pl.BlockSpec((R, tn), lambda j: (0, j)),
        out_shape=jax.ShapeDtypeStruct((R, N), F32),
        compiler_params=_cparams(("parallel",), 32),
        name="ada_mod",
    )(cs, w, b.reshape(1, N))


def _nmm_kernel(x_ref, g_ref, sc_ref, sh_ref, w_ref, o_ref, h_ref):
    @pl.when(pl.program_id(2) == 0)
    def _():
        h_ref[...] = _rms_mod(x_ref[0], g_ref[...], sc_ref[0], sh_ref[0]).astype(BF16)

    o_ref[0] = _bdot(h_ref[...], w_ref[...]).astype(o_ref.dtype)


def _norm_mod_matmul(x, g, sc, sh, w):
    Bx, Lx, D = x.shape
    N = w.shape[1]
    tm = min(Lx, 1024)
    tn = 1280 if N % 1280 == 0 else 1024
    return pl.pallas_call(
        _nmm_kernel,
        grid=(Bx, Lx // tm, N // tn),
        in_specs=[pl.BlockSpec((1, tm, D), lambda b, i, j: (b, i, 0)),
                  pl.BlockSpec((1, D), lambda b, i, j: (0, 0)),
                  pl.BlockSpec((1, 1, D), lambda b, i, j: (b, 0, 0)),
                  pl.BlockSpec((1, 1, D), lambda b, i, j: (b, 0, 0)),
                  pl.BlockSpec((D, tn), lambda b, i, j: (0, j))],
        out_specs=pl.BlockSpec((1, tm, tn), lambda b, i, j: (b, i, j)),
        out_shape=jax.ShapeDtypeStruct((Bx, Lx, N), BF16),
        scratch_shapes=[pltpu.VMEM((tm, D), BF16)],
        compiler_params=_cparams(("parallel", "parallel", "arbitrary"), 40),
        name="norm_mod_proj",
    )(x, g.reshape(1, D), sc, sh, w)


def _head_norm(x, gn, bd):
    hi, lo = _split(x * x)
    ms = _bdot(hi, bd) + _bdot(lo, bd)
    return x * lax.rsqrt(ms + EPS) * gn


def _rope(x, cos, sin_signed):
    lane = lax.broadcasted_iota(jnp.int32, x.shape, 1)
    quarter = HEAD_DIM // 4
    partner = jnp.where((lane % (2 * quarter)) < quarter,
                        pltpu.roll(x, x.shape[1] - quarter, 1), pltpu.roll(x, quarter, 1))
    return x * cos + partner * sin_signed


def _qk_rope_kernel(q_ref, k_ref, qn_ref, kn_ref, cos_ref, sin_ref, bd_ref, qr_ref, qp_ref, kr_ref):
    bd = bd_ref[...]
    cos = cos_ref[...]
    sin = sin_ref[...]
    q = _head_norm(q_ref[0].astype(F32), qn_ref[...], bd) * (HEAD_DIM ** -0.5)
    k = _head_norm(k_ref[0].astype(F32), kn_ref[...], bd)
    qp_ref[0] = q.astype(BF16)
    qr_ref[0] = _rope(q, cos, sin).astype(BF16)
    kr_ref[0] = _rope(k, cos, sin).astype(BF16)


def _qk_plain_kernel(q_ref, k_ref, qn_ref, kn_ref, bd_ref, qp_ref, kp_ref):
    bd = bd_ref[...]
    qp_ref[0] = (_head_norm(q_ref[0].astype(F32), qn_ref[...], bd) * (HEAD_DIM ** -0.5)).astype(BF16)
    kp_ref[0] = _head_norm(k_ref[0].astype(F32), kn_ref[...], bd).astype(BF16)


def _head_block_diag():
    r = jnp.arange(NA_WIDTH) // HEAD_DIM
    return jnp.where(r[:, None] == r[None, :], 1.0 / HEAD_DIM, 0.0).astype(BF16)


def _qk_prep(p, qcol, kcol, qn, kn, rope_tabs):
    Bx, Lx, _ = p.shape
    W = NA_WIDTH
    tm = min(Lx, 512)
    tok = lambda c: pl.BlockSpec((1, tm, W), lambda b, i: (b, i, c))
    vec = pl.BlockSpec((1, W), lambda b, i: (0, 0))
    mat = pl.BlockSpec((W, W), lambda b, i: (0, 0))
    out = pl.BlockSpec((1, tm, W), lambda b, i: (b, i, 0))
    osd = jax.ShapeDtypeStruct((Bx, Lx, W), BF16)
    qn_t = jnp.tile(qn, NA_HEADS).reshape(1, W)
    kn_t = jnp.tile(kn, NA_HEADS).reshape(1, W)
    bd = _head_block_diag()
    if rope_tabs is None:
        return pl.pallas_call(
            _qk_plain_kernel, grid=(Bx, Lx // tm),
            in_specs=[tok(qcol), tok(kcol), vec, vec, mat],
            out_specs=[out, out], out_shape=[osd, osd],
            compiler_params=_cparams(("parallel", "parallel"), 32), name="qk_norm",
        )(p, p, qn_t, kn_t, bd)
    cos, sin = rope_tabs
    tab = pl.BlockSpec((tm, W), lambda b, i: (i, 0))
    return pl.pallas_call(
        _qk_rope_kernel, grid=(Bx, Lx // tm),
        in_specs=[tok(qcol), tok(kcol), vec, vec, tab, tab, mat],
        out_specs=[out, out, out], out_shape=[osd, osd, osd],
        compiler_params=_cparams(("parallel", "parallel"), 32), name="qk_norm_rope",
    )(p, p, qn_t, kn_t, cos, sin, bd)


def _rope_tables(L):
    quarter = HEAD_DIM // 4
    freqs = ROPE_THETA ** (-jnp.arange(quarter, dtype=F32) / quarter)
    pos = jnp.arange(L)
    rows, cols = (pos // GRID_W).astype(F32), (pos % GRID_W).astype(F32)
    d = jnp.arange(NA_WIDTH) % HEAD_DIM
    p = jnp.where((d < HEAD_DIM // 2)[None, :], rows[:, None], cols[:, None])
    ang = p * freqs[d % quarter][None, :]
    sign = jnp.where((d % (2 * quarter)) < quarter, -1.0, 1.0)[None, :]
    return jnp.cos(ang), jnp.sin(ang) * sign


def _softmax_pv(s_list, v_list):
    m = s_list[0].max(axis=-1, keepdims=True)
    for s in s_list[1:]:
        m = jnp.maximum(m, s.max(axis=-1, keepdims=True))
    l = 0.0
    o = 0.0
    for s, v in zip(s_list, v_list):
        p = jnp.exp(s - m)
        l = l + p.sum(axis=-1, keepdims=True)
        o = o + _bdot(p.astype(BF16), v)
    return o / l


def _qkt(q, k):
    return lax.dot_general(q, k, (((1,), (1,)), ((), ())), preferred_element_type=F32)


def _head_lanes(x, lane, hh):
    return jnp.where((lane // HEAD_DIM) == hh, x, 0.0).astype(BF16)


def _nbr_attn_kernel(q_ref, qp_ref, k_ref, v_ref, kc_ref, vc_ref, bias_ref, o_ref):
    i = pl.program_id(1)
    max_row0 = k_ref.shape[1] // GRID_W - K_ROWS
    k0 = pl.multiple_of(jnp.clip(Q_ROWS * i - WIN_H // 2, 0, max_row0) * GRID_W, 4 * GRID_W)
    nk = K_ROWS * GRID_W
    k = k_ref[0, pl.ds(k0, nk), :]
    v = v_ref[0, pl.ds(k0, nk), :].astype(BF16)
    kc = kc_ref[0]
    vc = vc_ref[0].astype(BF16)
    q = q_ref[0].astype(F32)
    qp = qp_ref[0].astype(F32)
    lane = lax.broadcasted_iota(jnp.int32, q.shape, 1)
    out = None
    for hh in range(ATT_W // HEAD_DIM):
        s_w = _qkt(_head_lanes(q, lane, hh), k) + bias_ref[hh, 0]
        s_c = _qkt(_head_lanes(qp, lane, hh), kc)
        o = _softmax_pv([s_w, s_c], [v, vc])
        out = o if out is None else jnp.where((lane // HEAD_DIM) == hh, o, out)
    o_ref[0] = out.astype(o_ref.dtype)


def _nbr_attention(q_rot, q_plain, k_rot, p, v_blk, kc, pc, vc_blk, bias):
    B, L, _ = q_rot.shape
    Lc = kc.shape[1]
    nq = Q_ROWS * GRID_W
    ni = L // nq
    return pl.pallas_call(
        _nbr_attn_kernel,
        grid=(NA_WIDTH // ATT_W, ni, B),
        in_specs=[pl.BlockSpec((1, nq, ATT_W), lambda h, i, b: (b, i, h)),
                  pl.BlockSpec((1, nq, ATT_W), lambda h, i, b: (b, i, h)),
                  pl.BlockSpec((1, L, ATT_W), lambda h, i, b: (b, 0, h)),
                  pl.BlockSpec((1, L, ATT_W), lambda h, i, b: (b, 0, v_blk + h)),
                  pl.BlockSpec((1, Lc, ATT_W), lambda h, i, b: (b, 0, h)),
                  pl.BlockSpec((1, Lc, ATT_W), lambda h, i, b: (b, 0, vc_blk + h)),
                  pl.BlockSpec((ATT_W // HEAD_DIM, 1, nq, K_ROWS * GRID_W), lambda h, i, b: (h, i, 0, 0))],
        out_specs=pl.BlockSpec((1, nq, ATT_W), lambda h, i, b: (b, i, h)),
        out_shape=jax.ShapeDtypeStruct((B, L, NA_WIDTH), BF16),
        compiler_params=_cparams(("parallel", "parallel", "arbitrary"), 48),
        name="nbr_attention",
    )(q_rot, q_plain, k_rot, p, kc, pc, bias)


def _ctx_attn_kernel(q_ref, k_ref, v_ref, o_ref):
    q = q_ref[0].astype(F32)
    k = k_ref[0]
    v = v_ref[0].astype(BF16)
    lane = lax.broadcasted_iota(jnp.int32, q.shape, 1)
    out = None
    for hh in range(ATT_W // HEAD_DIM):
        o = _softmax_pv([_qkt(_head_lanes(q, lane, hh), k)], [v])
        out = o if out is None else jnp.where((lane // HEAD_DIM) == hh, o, out)
    o_ref[0] = out.astype(o_ref.dtype)


def _ctx_attention(qc, kc, pc, vc_blk):
    B, Lc, _ = qc.shape
    blk = lambda off: pl.BlockSpec((1, Lc, ATT_W), lambda h, b: (b, 0, off + h))
    return pl.pallas_call(
        _ctx_attn_kernel,
        grid=(NA_WIDTH // ATT_W, B),
        in_specs=[blk(0), blk(0), blk(vc_blk)],
        out_specs=blk(0),
        out_shape=jax.ShapeDtypeStruct((B, Lc, NA_WIDTH), BF16),
        compiler_params=_cparams(("parallel", "parallel"), 32),
        name="ctx_attention",
    )(qc, kc, pc)


def _nbr_bias(rpb, L):
    R = L // GRID_W
    kh = min(WIN_H, R)
    qc = jnp.arange(GRID_W)[:, None]
    kcol = jnp.arange(GRID_W)[None, :]
    wstart = jnp.clip(qc - WIN_W // 2, 0, GRID_W - WIN_W)
    col_ok = (kcol >= wstart) & (kcol < wstart + WIN_W)
    dc = jnp.clip(kcol - qc, -(WIN_W - 1), WIN_W - 1) + (WIN_W - 1)
    t = jnp.where(col_ok[None, None], rpb[:, :, dc].astype(F32), NEG)
    t_cat = t.transpose(0, 2, 1, 3).reshape(NA_HEADS, GRID_W, (2 * WIN_H - 1) * GRID_W)
    rows = []
    for r in range(R):
        k0 = min(max(Q_ROWS * (r // Q_ROWS) - WIN_H // 2, 0), R - K_ROWS)
        rstart = min(max(r - kh // 2, 0), R - kh)
        a_lo = rstart - r + (WIN_H - 1)
        win = t_cat[:, :, a_lo * GRID_W:(a_lo + kh) * GRID_W]
        left = (rstart - k0) * GRID_W
        right = (K_ROWS - kh) * GRID_W - left
        rows.append(jnp.pad(win, ((0, 0), (0, 0), (left, right)), constant_values=NEG))
    return jnp.stack(rows, axis=1).reshape(NA_HEADS, R // Q_ROWS, Q_ROWS * GRID_W, K_ROWS * GRID_W)


def _short_conv(u, w, b):
    n = u.shape[0]
    row = lax.broadcasted_iota(jnp.int32, u.shape, 0)
    prev = jnp.where(row == 0, 0.0, pltpu.roll(u, 1, 0))
    nxt = jnp.where(row == n - 1, 0.0, pltpu.roll(u, n - 1, 0))
    return prev * w[0:1] + u * w[1:2] + nxt * w[2:3] + b


def _dft_mats(L):
    n = 2 * L
    f = jnp.arange(L, dtype=jnp.int32)[:, None]
    s = jnp.arange(L, dtype=jnp.int32)[None, :]
    ang = ((f * s) % n).astype(F32) * (2.0 * math.pi / n)
    alt = jnp.where(s % 2 == 0, 1.0, -1.0).astype(F32)
    c = jnp.cos(ang)
    sm = jnp.where(f == 0, alt, -jnp.sin(ang))
    return c.astype(BF16), sm.astype(BF16), sm.T.astype(BF16)


def _filter_features(L):
    pos = jnp.arange(L, dtype=F32)
    t = pos / max(L - 1, 1)
    w = 2.0 * math.pi * pos / L
    f = jnp.linspace(1e-4, HY_BANDS - 1, HY_BANDS, dtype=F32)
    z = jnp.concatenate([t[:, None], jnp.cos(f[None, :] * w[:, None]), -jnp.sin(f[None, :] * w[:, None])], axis=-1)
    return jnp.pad(z, ((0, 0), (0, LANES - z.shape[1])))


def _filter_kernel(z_ref, w1_ref, b1_ref, w2_ref, b2_ref, w3f_ref, w3b_ref, dl_ref, c_ref, s_ref, o_ref):
    L = z_ref.shape[0]
    hid = jnp.sin(HY_SIN_FREQ * (_dot3(z_ref[...], w1_ref[...]) + b1_ref[...]))
    hid = jnp.sin(HY_SIN_FREQ * (_dot3(hid, w2_ref[...]) + b2_ref[...]))
    hf = _dot3(hid, w3f_ref[...])
    hb = _dot3(hid, w3b_ref[...])
    row = lax.broadcasted_iota(jnp.int32, hf.shape, 0)
    t = row.astype(F32) / float(max(L - 1, 1))
    dec = jnp.exp(-t * dl_ref[...])
    hf = hf * dec
    hb = jnp.where(row == 0, 0.0, hb * dec)
    inv = 1.0 / (jnp.sum(jnp.abs(hf), axis=0, keepdims=True) + jnp.sum(jnp.abs(hb), axis=0, keepdims=True) + EPS)
    h = L // 2
    tc = hf.shape[1]
    hf = hf * inv
    hb = hb * inv
    rowh = lax.broadcasted_iota(jnp.int32, (h, tc), 0)
    f0 = hf[:h]
    x = jnp.concatenate([f0, jnp.where(rowh == 0, 0.0, f0), hf[h:], hb[:h], hb[h:]], axis=1).astype(BF16)
    re = _bdot(c_ref[...], x)
    im = _bdot(s_ref[...], x)
    rf0, rf0z, rf1, rb0, rb1 = [re[:, i * tc:(i + 1) * tc] for i in range(5)]
    if0, if0z, if1, ib0, ib1 = [im[:, i * tc:(i + 1) * tc] for i in range(5)]
    sgn = jnp.where(rowh % 2 == 0, 1.0, -1.0)
    conj = lambda i: jnp.where(rowh == 0, i, -i)
    blocks = [(rf0 + rb0, if0 + conj(ib0)),
              (rf1 + sgn * rf0z, if1 + sgn * if0z),
              (rb1 + sgn * rb0, conj(ib1) + sgn * conj(ib0))]
    alpha = jnp.where(rowh == 0, 1.0 / L, 2.0 / L)
    for i, (kr, ki) in enumerate(blocks):
        o_ref[3 * i] = kr * alpha
        o_ref[3 * i + 1] = jnp.where(rowh == 0, 0.0, ki) * alpha
        o_ref[3 * i + 2] = jnp.where(rowh == 0, ki, kr) * alpha


def _pad2(a, r, c):
    return jnp.pad(a, ((0, r - a.shape[0]), (0, c - a.shape[1])))


def _hyena_filter(L, w1, b1, w2, b2, w3, dft):
    c, s, _ = dft
    n = HY_ORDER * HY_WIDTH
    tc = 256
    z = _filter_features(L)
    w1p = _pad2(w1, LANES, LANES)
    w2p = _pad2(w2, LANES, LANES)
    w3p = _pad2(w3, LANES, 2 * n)
    b1p = _pad2(b1[None, :], 1, LANES)
    b2p = _pad2(b2[None, :], 1, LANES)
    deltas = jnp.abs(jnp.linspace(HY_MIN_DECAY, HY_MAX_DECAY, HY_WIDTH, dtype=F32)).reshape(1, HY_WIDTH)
    full = lambda shape: pl.BlockSpec(shape, lambda j: (0, 0))
    const = lambda shape: pl.BlockSpec(shape, lambda j: (0, 0), pipeline_mode=pl.Buffered(1))
    h = L // 2
    return pl.pallas_call(
        _filter_kernel,
        grid=(n // tc,),
        in_specs=[full((L, LANES)), full((LANES, LANES)), full((1, LANES)), full((LANES, LANES)), full((1, LANES)),
                  pl.BlockSpec((LANES, tc), lambda j: (0, j)),
                  pl.BlockSpec((LANES, tc), lambda j: (0, n // tc + j)),
                  pl.BlockSpec((1, tc), lambda j: (0, j % (HY_WIDTH // tc))),
                  const((h, h)), const((h, h))],
        out_specs=pl.BlockSpec((9, h, tc), lambda j: (0, 0, j)),
        out_shape=jax.ShapeDtypeStruct((9, h, n), F32),
        compiler_params=_cparams(("parallel",), 56),
        name="hyena_filter",
    )(z, w1p, b1p, w2p, b2p, w3p, w3p, deltas, c, s)


def _hyena_order_kernel(z_ref, g_ref, swz_ref, sbz_ref, swg_ref, sbg_ref, sk_ref, c_ref, s_ref, st_ref,
                        f_ref, o_ref, *, nchunk, conv_z):
    z = z_ref[0].astype(F32)
    if conv_z:
        z = _short_conv(z, swz_ref[...], sbz_ref[...])
    gate = _short_conv(g_ref[0].astype(F32), swg_ref[...], sbg_ref[...])
    L, tc = z.shape
    h = L // 2
    zb = z.astype(BF16)
    zcat = jnp.concatenate([zb[:h], zb[h:]], axis=1)
    fc = h // nchunk
    conv = None
    for ci in range(nchunk):
        sl = pl.ds(ci * fc, fc)
        zr = _bdot(c_ref[sl, :], zcat)
        zi = _bdot(s_ref[sl, :], zcat)
        ztr, zbr, zti, zbi = zr[:, :tc], zr[:, tc:], zi[:, :tc], zi[:, tc:]
        p0, q0, p20, p1, q1, p21, pm, qm, p2m = [f_ref[i, sl, :] for i in range(9)]
        ytr = ztr * p0 - zti * q0 + zbr * pm - zbi * qm
        yti = ztr * q0 + zti * p20 + zbr * qm + zbi * p2m
        ybr = ztr * p1 - zti * q1 + zbr * p0 - zbi * q0
        ybi = ztr * q1 + zti * p21 + zbr * q0 + zbi * p20
        yr = jnp.concatenate([ytr, ybr], axis=1).astype(BF16)
        yi = jnp.concatenate([yti, ybi], axis=1).astype(BF16)
        part = _bdot(c_ref[:, sl], yr) + _bdot(st_ref[:, sl], yi)
        conv = part if conv is None else conv + part
    conv = jnp.concatenate([conv[:, :tc], conv[:, tc:]], axis=0)
    o_ref[0] = (gate * (conv + z * sk_ref[...])).astype(o_ref.dtype)


def _hyena_order(zsrc, zblk, conv_z, p, hy_blk, part_z, part_g, short_w, short_b, skip, filt, order, dft, out_dtype):
    Bx, Lx, _ = zsrc.shape
    c, s, st = dft
    tc = 256
    nct = HY_WIDTH // tc
    taps = short_w.shape[0]
    h = Lx // 2
    const = lambda shape: pl.BlockSpec(shape, lambda j, b: (0, 0), pipeline_mode=pl.Buffered(1))
    fspec = pl.BlockSpec((9, h, tc), lambda j, b: (0, 0, order * nct + j), pipeline_mode=pl.Buffered(1))
    wspec = lambda part: pl.BlockSpec((taps, tc), lambda j, b: (0, part * nct + j))
    bspec = lambda part: pl.BlockSpec((1, tc), lambda j, b: (0, part * nct + j))
    return pl.pallas_call(
        functools.partial(_hyena_order_kernel, nchunk=max(1, h // 512), conv_z=conv_z),
        grid=(nct, Bx),
        in_specs=[pl.BlockSpec((1, Lx, tc), lambda j, b: (b, 0, zblk + j)),
                  pl.BlockSpec((1, Lx, tc), lambda j, b: (b, 0, hy_blk + part_g * nct + j)),
                  wspec(part_z), bspec(part_z), wspec(part_g), bspec(part_g),
                  pl.BlockSpec((1, tc), lambda j, b: (0, j)),
                  const((h, h)), const((h, h)), const((h, h)),
                  fspec],
        out_specs=pl.BlockSpec((1, Lx, tc), lambda j, b: (b, 0, j)),
        out_shape=jax.ShapeDtypeStruct((Bx, Lx, HY_WIDTH), out_dtype),
        compiler_params=_cparams(("parallel", "arbitrary"), 48),
        name="hyena_long_conv",
    )(zsrc, p, short_w, short_b.reshape(1, -1), short_w, short_b.reshape(1, -1), skip.reshape(1, HY_WIDTH),
      c, s, st, filt)


def _hyena(p, hy_blk, short_w, short_b, filt, skip, dft):
    z1 = _hyena_order(p, hy_blk, True, p, hy_blk, 0, 1, short_w, short_b, skip[0], filt, 0, dft, F32)
    return _hyena_order(z1, 0, False, p, hy_blk, 0, 2, short_w, short_b, skip[1], filt, 1, dft, BF16)


def _merge_kernel(a_ref, h_ref, ga_ref, gb_ref, x_ref, g1_ref, wa_ref, wb_ref, wo_ref, o_ref):
    ga = _sigmoid(ga_ref[0].astype(F32))
    gb = _sigmoid(gb_ref[0].astype(F32))
    y = ga * _bdot(a_ref[0], wa_ref[...]) + gb * _bdot(h_ref[0], wb_ref[...])
    o_ref[0] = x_ref[0] + g1_ref[0] * _bdot(y.astype(BF16), wo_ref[...])


def _merge(attn, hyena, p, gate_blk, x, g1, wa, wb, wo):
    Bx, Lx, D = x.shape
    tm = min(Lx, 512)
    W = attn.shape[2]
    const = lambda shape: pl.BlockSpec(shape, lambda b, i: (0, 0))
    return pl.pallas_call(
        _merge_kernel,
        grid=(Bx, Lx // tm),
        in_specs=[pl.BlockSpec((1, tm, W), lambda b, i: (b, i, 0)),
                  pl.BlockSpec((1, tm, W), lambda b, i: (b, i, 0)),
                  pl.BlockSpec((1, tm, D), lambda b, i: (b, i, gate_blk)),
                  pl.BlockSpec((1, tm, D), lambda b, i: (b, i, gate_blk + 1)),
                  pl.BlockSpec((1, tm, D), lambda b, i: (b, i, 0)),
                  pl.BlockSpec((1, 1, D), lambda b, i: (b, 0, 0)),
                  const((W, D)), const((W, D)), const((D, D))],
        out_specs=pl.BlockSpec((1, tm, D), lambda b, i: (b, i, 0)),
        out_shape=jax.ShapeDtypeStruct((Bx, Lx, D), F32),
        compiler_params=_cparams(("parallel", "parallel"), 48),
        name="merge_out_proj",
    )(attn, hyena, p, p, x, g1, wa, wb, wo)


MOE_TILE = 1024
MOE_CHUNK = 256
GROUP_LANE = N_EXPERTS
RANK_LANE = N_EXPERTS + 1


def _router_kernel(x_ref, g_ref, sc_ref, sh_ref, wr_ref, br_ref, ht_ref, cw_ref, cwt_ref):
    h = _rms_mod(x_ref[0], g_ref[...], sc_ref[0], sh_ref[0])
    ht_ref[0] = h.T.astype(BF16)
    logits = _dot3(h, wr_ref[...]) + br_ref[...]
    lane = lax.broadcasted_iota(jnp.int32, logits.shape, 1)
    lane_f = lane.astype(F32)
    big = float(LANES)
    is_g = (lane >= N_EXPERTS) & (lane < N_EXPERTS + N_GROUPS)
    gl = jnp.where(is_g, logits, NEG)
    gmax = gl.max(axis=-1, keepdims=True)
    gp = 1.0 / jnp.where(is_g, jnp.exp(gl - gmax), 0.0).sum(axis=-1, keepdims=True)
    gidx = jnp.where(is_g & (gl == gmax), lane_f - N_EXPERTS, big).min(axis=-1, keepdims=True)
    in_grp = (lane < N_EXPERTS) & ((lane // EXPERTS_PER_GROUP).astype(F32) == gidx)
    el = jnp.where(in_grp, logits, NEG)
    v1 = el.max(axis=-1, keepdims=True)
    i1 = jnp.where(in_grp & (el == v1), lane_f, big).min(axis=-1, keepdims=True)
    rest = in_grp & (lane_f != i1)
    el2 = jnp.where(rest, logits, NEG)
    v2 = el2.max(axis=-1, keepdims=True)
    i2 = jnp.where(rest & (el2 == v2), lane_f, big).min(axis=-1, keepdims=True)
    e2 = jnp.exp(v2 - v1)
    w1 = gp / (1.0 + e2)
    cw = jnp.where(lane_f == i1, w1, jnp.where(lane_f == i2, w1 * e2, 0.0))
    tm = h.shape[0]
    onehot = jnp.where((lane_f == gidx) & (lane < N_GROUPS), 1.0, 0.0)
    tri = lax.broadcasted_iota(jnp.int32, (tm, tm), 1) <= lax.broadcasted_iota(jnp.int32, (tm, tm), 0)
    cum = _bdot(jnp.where(tri, 1.0, 0.0).astype(BF16), onehot.astype(BF16))
    rank = (onehot * cum).sum(axis=-1, keepdims=True) - 1.0
    rec = jnp.where(lane == GROUP_LANE, gidx, jnp.where(lane == RANK_LANE, rank, cw))
    cw_ref[0] = rec
    cwt_ref[0] = rec.T


def _router(x, g, sc, sh, w_group, b_group, w_router, b_router):
    Bx, Lx, D = x.shape
    tm = MOE_TILE
    wr = _pad2(jnp.concatenate([w_router, w_group], axis=1), D, LANES)
    br = _pad2(jnp.concatenate([b_router, b_group])[None, :], 1, LANES)
    return pl.pallas_call(
        _router_kernel,
        grid=(Bx, Lx // tm),
        in_specs=[pl.BlockSpec((1, tm, D), lambda b, i: (b, i, 0)),
                  pl.BlockSpec((1, D), lambda b, i: (0, 0)),
                  pl.BlockSpec((1, 1, D), lambda b, i: (b, 0, 0)),
                  pl.BlockSpec((1, 1, D), lambda b, i: (b, 0, 0)),
                  pl.BlockSpec((D, LANES), lambda b, i: (0, 0)),
                  pl.BlockSpec((1, LANES), lambda b, i: (0, 0))],
        out_specs=[pl.BlockSpec((1, D, tm), lambda b, i: (b, 0, i)),
                   pl.BlockSpec((1, tm, LANES), lambda b, i: (b, i, 0)),
                   pl.BlockSpec((1, LANES, tm), lambda b, i: (b, 0, i))],
        out_shape=[jax.ShapeDtypeStruct((Bx, D, Lx), BF16), jax.ShapeDtypeStruct((Bx, Lx, LANES), F32),
                   jax.ShapeDtypeStruct((Bx, LANES, Lx), F32)],
        compiler_params=_cparams(("parallel", "parallel"), 48),
        name="moe_router",
    )(x, g.reshape(1, D), sc, sh, wr, br)


def _moe_kernel(cnt_ref, ht_ref, cw_ref, cwt_ref, x_ref, g2_ref, w1t_ref, w3t_ref, w2t_ref, o_ref, acc_ref):
    b, i, g = pl.program_id(0), pl.program_id(1), pl.program_id(2)

    @pl.when(g == 0)
    def _():
        acc_ref[...] = jnp.zeros_like(acc_ref)

    n = cnt_ref[(b * pl.num_programs(1) + i) * N_GROUPS + g]
    gf = g.astype(F32)
    tm = acc_ref.shape[1]
    c = MOE_CHUNK
    sel_col = jnp.where(cw_ref[0, :, GROUP_LANE:GROUP_LANE + 1] == gf, cw_ref[0, :, RANK_LANE:RANK_LANE + 1], -1.0)
    sel_row = jnp.where(cwt_ref[0, GROUP_LANE:GROUP_LANE + 1, :] == gf, cwt_ref[0, RANK_LANE:RANK_LANE + 1, :], -1.0)
    e0 = pl.multiple_of(g * EXPERTS_PER_GROUP, EXPERTS_PER_GROUP)
    cw_hi, cw_lo = _split(cwt_ref[0, pl.ds(e0, EXPERTS_PER_GROUP), :])
    lane_r = lax.broadcasted_iota(jnp.int32, (tm, c), 1).astype(F32)
    sub_r = lax.broadcasted_iota(jnp.int32, (c, tm), 0).astype(F32)

    def chunk(k, carry):
        r0 = (k * c).astype(F32)
        pkt = jnp.where(sel_col - r0 == lane_r, 1.0, 0.0).astype(BF16)
        pk = jnp.where(sel_row - r0 == sub_r, 1.0, 0.0).astype(BF16)
        xst = _bdot(ht_ref[0], pkt).astype(BF16)
        cws = _bdot(cw_hi, pkt) + _bdot(cw_lo, pkt)
        at = _bdot(w1t_ref[0], xst)
        bt = _bdot(w3t_ref[0], xst)
        out = None
        for e in range(EXPERTS_PER_GROUP):
            rows = slice(e * D_EXPERT, (e + 1) * D_EXPERT)
            a = at[rows]
            hid = (a * _sigmoid(a)) * bt[rows] * cws[e:e + 1]
            part = _bdot(w2t_ref[0, :, rows], hid.astype(BF16))
            out = part if out is None else out + part
        acc_ref[...] += _bdot(out.astype(BF16), pk)
        return carry

    lax.fori_loop(0, (n + c - 1) // c, chunk, 0)

    @pl.when(g == N_GROUPS - 1)
    def _():
        o_ref[0] = x_ref[0] + g2_ref[0] * acc_ref[...].T


def _moe(ht, cw, cwt, x, g2, w1t, w3t, w2t):
    Bx, Lx, D = x.shape
    tm = MOE_TILE
    nt = Lx // tm
    ef = EXPERTS_PER_GROUP * D_EXPERT
    gid = cw[..., GROUP_LANE].reshape(Bx, nt, tm, 1)
    counts = jnp.sum(gid == jnp.arange(N_GROUPS, dtype=F32), axis=2).astype(jnp.int32).reshape(-1)
    once = pl.Buffered(1)
    grid_spec = pltpu.PrefetchScalarGridSpec(
        num_scalar_prefetch=1,
        grid=(Bx, nt, N_GROUPS),
        in_specs=[pl.BlockSpec((1, D, tm), lambda b, i, g, cnt: (b, 0, i), pipeline_mode=once),
                  pl.BlockSpec((1, tm, LANES), lambda b, i, g, cnt: (b, i, 0)),
                  pl.BlockSpec((1, LANES, tm), lambda b, i, g, cnt: (b, 0, i)),
                  pl.BlockSpec((1, tm, D), lambda b, i, g, cnt: (b, i, 0), pipeline_mode=once),
                  pl.BlockSpec((1, 1, D), lambda b, i, g, cnt: (b, 0, 0)),
                  pl.BlockSpec((1, ef, D), lambda b, i, g, cnt: (g, 0, 0)),
                  pl.BlockSpec((1, ef, D), lambda b, i, g, cnt: (g, 0, 0)),
                  pl.BlockSpec((1, D, ef), lambda b, i, g, cnt: (g, 0, 0))],
        out_specs=pl.BlockSpec((1, tm, D), lambda b, i, g, cnt: (b, i, 0)),
        scratch_shapes=[pltpu.VMEM((D, tm), F32)])
    return pl.pallas_call(
        _moe_kernel,
        grid_spec=grid_spec,
        out_shape=jax.ShapeDtypeStruct((Bx, Lx, D), F32),
        compiler_params=_cparams(("parallel", "parallel", "arbitrary"), 56),
        name="moe_experts",
    )(counts, ht, cw, cwt, x, g2, w1t, w3t, w2t)


def _layer(x, xc, mods, modc, last, lw, consts):
    B, L, D = x.shape
    Lc = xc.shape[1]
    sh1, sc1, g1, sh2, sc2, g2 = mods
    sh1c, sc1c, g1c, sh2c, sc2c, g2c = modc
    w_in = lw["w_in"]
    hy_blk = 3 * NA_WIDTH // 256
    gate_blk = (3 * NA_WIDTH + (HY_ORDER + 1) * HY_WIDTH) // D
    v_blk = 2 * NA_WIDTH // ATT_W

    p = _norm_mod_matmul(x, lw["norm_mix"], sc1, sh1, w_in)
    q_rot, q_plain, k_rot = _qk_prep(p, 0, 1, lw["q_norm"], lw["k_norm"], consts["rope"])
    if last:
        pc = _norm_mod_matmul(xc, lw["norm_mix"], sc1c, sh1c, w_in[:, NA_WIDTH:3 * NA_WIDTH])
        _, kc = _qk_prep(pc, 0, 0, lw["q_norm"], lw["k_norm"], None)
        vc_blk = NA_WIDTH // ATT_W
    else:
        pc = _norm_mod_matmul(xc, lw["norm_mix"], sc1c, sh1c, w_in)
        qc, kc = _qk_prep(pc, 0, 1, lw["q_norm"], lw["k_norm"], None)
        vc_blk = v_blk
    attn = _nbr_attention(q_rot, q_plain, k_rot, p, v_blk, kc, pc, vc_blk, _nbr_bias(lw["rpb"], L))
    flt = (lw["flt_w1"], lw["flt_b1"], lw["flt_w2"], lw["flt_b2"], lw["flt_w3"])
    filt = _hyena_filter(L, *flt, consts["dft"])
    hyena = _hyena(p, hy_blk, lw["short_w"], lw["short_b"], filt, lw["hy_skip"], consts["dft"])
    x = _merge(attn, hyena, p, gate_blk, x, g1, lw["w_br_a"], lw["w_br_b"], lw["w_out"])
    rw = (lw["w_group"], lw["b_group"], lw["w_router"], lw["b_router"])
    ew = (lw["moe_w1t"], lw["moe_w3t"], lw["moe_w2t"])
    x = _moe(*_router(x, lw["norm_ffn"], sc2, sh2, *rw), x, g2, *ew)
    if last:
        return x, xc

    attn_c = _ctx_attention(qc, kc, pc, vc_blk)
    filt_c = _hyena_filter(Lc, *flt, consts["dft_c"])
    hyena_c = _hyena(pc, hy_blk, lw["short_w"], lw["short_b"], filt_c, lw["hy_skip"], consts["dft_c"])
    xc = _merge(attn_c, hyena_c, pc, gate_blk, xc, g1c, lw["w_br_a"], lw["w_br_b"], lw["w_out"])
    xf = xc.reshape(1, B * Lc, D)
    xf = _moe(*_router(xf, lw["norm_ffn"], sc2c[:1], sh2c[:1], *rw), xf, g2c[:1], *ew)
    return x, xf.reshape(B, Lc, D)


def kernel(x, c, ctx, c_ctx, ada_w, ada_b, norm_mix, norm_ffn, w_in, q_norm, k_norm, rpb, short_w, short_b, flt_w1, flt_b1, flt_w2, flt_b2, flt_w3, hy_skip, w_br_a, w_br_b, w_out, w_group, b_group, w_router, b_router, moe_w1, moe_w3, moe_w2):
    B, L, D = x.shape
    Lc = ctx.shape[1]
    depth = ada_w.shape[0]
    consts = {"rope": _rope_tables(L), "dft": _dft_mats(L // 2), "dft_c": _dft_mats(Lc // 2)}
    rows = 8 * ((B + 1 + 7) // 8)
    cs = jnp.pad(jnp.concatenate([c, c_ctx[None, :]], axis=0), ((0, rows - B - 1), (0, 0)))
    xc = ctx
    for i in range(depth):
        mod = _ada(cs, ada_w[i], ada_b[i])
        mods = [m.reshape(B, 1, D) for m in jnp.split(mod[:B], 6, axis=-1)]
        modc = [jnp.broadcast_to(m.reshape(1, 1, D), (B, 1, D)) for m in jnp.split(mod[B], 6, axis=-1)]
        ef = EXPERTS_PER_GROUP * D_EXPERT
        lw = {
            "norm_mix": norm_mix[i], "norm_ffn": norm_ffn[i], "w_in": w_in[i].astype(BF16),
            "q_norm": q_norm[i], "k_norm": k_norm[i], "rpb": rpb[i],
            "short_w": short_w[i], "short_b": short_b[i],
            "flt_w1": flt_w1[i], "flt_b1": flt_b1[i], "flt_w2": flt_w2[i], "flt_b2": flt_b2[i], "flt_w3": flt_w3[i],
            "hy_skip": hy_skip[i],
            "w_br_a": w_br_a[i].astype(BF16), "w_br_b": w_br_b[i].astype(BF16), "w_out": w_out[i].astype(BF16),
            "w_group": w_group[i], "b_group": b_group[i], "w_router": w_router[i], "b_router": b_router[i],
            "moe_w1t": moe_w1[i].astype(BF16).transpose(0, 1, 3, 2).reshape(N_GROUPS, ef, D),
            "moe_w3t": moe_w3[i].astype(BF16).transpose(0, 1, 3, 2).reshape(N_GROUPS, ef, D),
            "moe_w2t": moe_w2[i].astype(BF16).transpose(0, 3, 1, 2).reshape(N_GROUPS, D, ef),
        }
        x, xc = _layer(x, xc, mods, modc, i == depth - 1, lw, consts)
    return x
```

```python
import functools
import math

import jax
import jax.numpy as jnp
from jax import lax
from jax.experimental import pallas as pl
from jax.experimental.pallas import tpu as pltpu

F32 = jnp.float32
BF16 = jnp.bfloat16

D_MODEL = 1024
GRID_W = 64
NA_HEADS = 8
HEAD_DIM = 64
NA_WIDTH = NA_HEADS * HEAD_DIM
WIN_H = 8
WIN_W = 16
ROPE_THETA = 100.0
HY_WIDTH = 512
HY_ORDER = 2
HY_BANDS = 16
HY_SIN_FREQ = 1.0
HY_MAX_DECAY = math.log(1e-2) / 0.3
HY_MIN_DECAY = math.log(1e-2) / 1.5
N_GROUPS = 4
EXPERTS_PER_GROUP = 8
N_EXPERTS = N_GROUPS * EXPERTS_PER_GROUP
D_EXPERT = 256
EPS = 1e-6
NEG = -1e30

LANES = 128
V7X_VMEM_BYTES = 64 * 1024 * 1024
ATT_W = 256
Q_ROWS = 4
K_ROWS = Q_ROWS + WIN_H


def _cparams(sem, vmem_mb):
    assert vmem_mb * 1024 * 1024 < V7X_VMEM_BYTES
    return pltpu.CompilerParams(dimension_semantics=sem, vmem_limit_bytes=vmem_mb * 1024 * 1024)


def _bdot(a, b):
    return jnp.dot(a, b, preferred_element_type=F32)


def _split(a):
    hi = a.astype(BF16)
    lo = (a - hi.astype(F32)).astype(BF16)
    return hi, lo


def _dot3(a, b):
    ah, al = _split(a)
    bh, bl = _split(b)
    return _bdot(ah, bh) + _bdot(ah, bl) + _bdot(al, bh)


def _sigmoid(x):
    return 1.0 / (1.0 + jnp.exp(-x))


def _rms_mod(x, g, sc, sh):
    ms = jnp.mean(x * x, axis=-1, keepdims=True)
    return (x * lax.rsqrt(ms + EPS) * g) * (1.0 + sc) + sh


def _ada_kernel(c_ref, w_ref, b_ref, o_ref):
    c = c_ref[...]
    o_ref[...] = _dot3(c * _sigmoid(c), w_ref[...]) + b_ref[...]


def _ada(cs, w, b):
    R, D = cs.shape
    N = w.shape[1]
    tn = 512
    return pl.pallas_call(
        _ada_kernel,
        grid=(N // tn,),
        in_specs=[pl.BlockSpec((R, D), lambda j: (0, 0)),
                  pl.BlockSpec((D, tn), lambda j: (0, j)),
                  pl.BlockSpec((1, tn), lambda j: (0, j))],
        out_specs=pl.BlockSpec((R, tn), lambda j: (0, j)),
        out_shape=jax.ShapeDtypeStruct((R, N), F32),
        compiler_params=_cparams(("parallel",), 32),
        name="ada_mod",
    )(cs, w, b.reshape(1, N))


def _nmm_kernel(x_ref, g_ref, sc_ref, sh_ref, w_ref, o_ref, h_ref):
    @pl.when(pl.program_id(2) == 0)
    def _():
        h_ref[...] = _rms_mod(x_ref[0], g_ref[...], sc_ref[0], sh_ref[0]).astype(BF16)

    o_ref[0] = _bdot(h_ref[...], w_ref[...]).astype(o_ref.dtype)


def _norm_mod_matmul(x, g, sc, sh, w):
    Bx, Lx, D = x.shape
    N = w.shape[1]
    tm = min(Lx, 1024)
    tn = 1280 if N % 1280 == 0 else 1024
    return pl.pallas_call(
        _nmm_kernel,
        grid=(Bx, Lx // tm, N // tn),
        in_specs=[pl.BlockSpec((1, tm, D), lambda b, i, j: (b, i, 0)),
                  pl.BlockSpec((1, D), lambda b, i, j: (0, 0)),
                  pl.BlockSpec((1, 1, D), lambda b, i, j: (b, 0, 0)),
                  pl.BlockSpec((1, 1, D), lambda b, i, j: (b, 0, 0)),
                  pl.BlockSpec((D, tn), lambda b, i, j: (0, j))],
        out_specs=pl.BlockSpec((1, tm, tn), lambda b, i, j: (b, i, j)),
        out_shape=jax.ShapeDtypeStruct((Bx, Lx, N), BF16),
        scratch_shapes=[pltpu.VMEM((tm, D), BF16)],
        compiler_params=_cparams(("parallel", "parallel", "arbitrary"), 40),
        name="norm_mod_proj",
    )(x, g.reshape(1, D), sc, sh, w)


def _head_norm(x, gn, bd):
    hi, lo = _split(x * x)
    ms = _bdot(hi, bd) + _bdot(lo, bd)
    return x * lax.rsqrt(ms + EPS) * gn


def _rope(x, cos, sin_signed):
    lane = lax.broadcasted_iota(jnp.int32, x.shape, 1)
    quarter = HEAD_DIM // 4
    partner = jnp.where((lane % (2 * quarter)) < quarter,
                        pltpu.roll(x, x.shape[1] - quarter, 1), pltpu.roll(x, quarter, 1))
    return x * cos + partner * sin_signed


def _qk_rope_kernel(q_ref, k_ref, qn_ref, kn_ref, cos_ref, sin_ref, bd_ref, qr_ref, qp_ref, kr_ref):
    bd = bd_ref[...]
    cos = cos_ref[...]
    sin = sin_ref[...]
    q = _head_norm(q_ref[0].astype(F32), qn_ref[...], bd) * (HEAD_DIM ** -0.5)
    k = _head_norm(k_ref[0].astype(F32), kn_ref[...], bd)
    qp_ref[0] = q.astype(BF16)
    qr_ref[0] = _rope(q, cos, sin).astype(BF16)
    kr_ref[0] = _rope(k, cos, sin).astype(BF16)


def _qk_plain_kernel(q_ref, k_ref, qn_ref, kn_ref, bd_ref, qp_ref, kp_ref):
    bd = bd_ref[...]
    qp_ref[0] = (_head_norm(q_ref[0].astype(F32), qn_ref[...], bd) * (HEAD_DIM ** -0.5)).astype(BF16)
    kp_ref[0] = _head_norm(k_ref[0].astype(F32), kn_ref[...], bd).astype(BF16)


def _head_block_diag():
    r = jnp.arange(NA_WIDTH) // HEAD_DIM
    return jnp.where(r[:, None] == r[None, :], 1.0 / HEAD_DIM, 0.0).astype(BF16)


def _qk_prep(p, qcol, kcol, qn, kn, rope_tabs):
    Bx, Lx, _ = p.shape
    W = NA_WIDTH
    tm = min(Lx, 512)
    tok = lambda c: pl.BlockSpec((1, tm, W), lambda b, i: (b, i, c))
    vec = pl.BlockSpec((1, W), lambda b, i: (0, 0))
    mat = pl.BlockSpec((W, W), lambda b, i: (0, 0))
    out = pl.BlockSpec((1, tm, W), lambda b, i: (b, i, 0))
    osd = jax.ShapeDtypeStruct((Bx, Lx, W), BF16)
    qn_t = jnp.tile(qn, NA_HEADS).reshape(1, W)
    kn_t = jnp.tile(kn, NA_HEADS).reshape(1, W)
    bd = _head_block_diag()
    if rope_tabs is None:
        return pl.pallas_call(
            _qk_plain_kernel, grid=(Bx, Lx // tm),
            in_specs=[tok(qcol), tok(kcol), vec, vec, mat],
            out_specs=[out, out], out_shape=[osd, osd],
            compiler_params=_cparams(("parallel", "parallel"), 32), name="qk_norm",
        )(p, p, qn_t, kn_t, bd)
    cos, sin = rope_tabs
    tab = pl.BlockSpec((tm, W), lambda b, i: (i, 0))
    return pl.pallas_call(
        _qk_rope_kernel, grid=(Bx, Lx // tm),
        in_specs=[tok(qcol), tok(kcol), vec, vec, tab, tab, mat],
        out_specs=[out, out, out], out_shape=[osd, osd, osd],
        compiler_params=_cparams(("parallel", "parallel"), 32), name="qk_norm_rope",
    )(p, p, qn_t, kn_t, cos, sin, bd)


def _rope_tables(L):
    quarter = HEAD_DIM // 4
    freqs = ROPE_THETA ** (-jnp.arange(quarter, dtype=F32) / quarter)
    pos = jnp.arange(L)
    rows, cols = (pos // GRID_W).astype(F32), (pos % GRID_W).astype(F32)
    d = jnp.arange(NA_WIDTH) % HEAD_DIM
    p = jnp.where((d < HEAD_DIM // 2)[None, :], rows[:, None], cols[:, None])
    ang = p * freqs[d % quarter][None, :]
    sign = jnp.where((d % (2 * quarter)) < quarter, -1.0, 1.0)[None, :]
    return jnp.cos(ang), jnp.sin(ang) * sign


def _softmax_pv(s_list, v_list):
    m = s_list[0].max(axis=-1, keepdims=True)
    for s in s_list[1:]:
        m = jnp.maximum(m, s.max(axis=-1, keepdims=True))
    l = 0.0
    o = 0.0
    for s, v in zip(s_list, v_list):
        p = jnp.exp(s - m)
        l = l + p.sum(axis=-1, keepdims=True)
        o = o + _bdot(p.astype(BF16), v)
    return o / l


def _qkt(q, k):
    return lax.dot_general(q, k, (((1,), (1,)), ((), ())), preferred_element_type=F32)


def _head_lanes(x, lane, hh):
    return jnp.where((lane // HEAD_DIM) == hh, x, 0.0).astype(BF16)


def _nbr_attn_kernel(q_ref, qp_ref, k_ref, v_ref, kc_ref, vc_ref, bias_ref, o_ref):
    i = pl.program_id(1)
    max_row0 = k_ref.shape[1] // GRID_W - K_ROWS
    k0 = pl.multiple_of(jnp.clip(Q_ROWS * i - WIN_H // 2, 0, max_row0) * GRID_W, 4 * GRID_W)
    nk = K_ROWS * GRID_W
    k = k_ref[0, pl.ds(k0, nk), :]
    v = v_ref[0, pl.ds(k0, nk), :].astype(BF16)
    kc = kc_ref[0]
    vc = vc_ref[0].astype(BF16)
    q = q_ref[0].astype(F32)
    qp = qp_ref[0].astype(F32)
    lane = lax.broadcasted_iota(jnp.int32, q.shape, 1)
    out = None
    for hh in range(ATT_W // HEAD_DIM):
        s_w = _qkt(_head_lanes(q, lane, hh), k) + bias_ref[hh, 0]
        s_c = _qkt(_head_lanes(qp, lane, hh), kc)
        o = _softmax_pv([s_w, s_c], [v, vc])
        out = o if out is None else jnp.where((lane // HEAD_DIM) == hh, o, out)
    o_ref[0] = out.astype(o_ref.dtype)


def _nbr_attention(q_rot, q_plain, k_rot, p, v_blk, kc, pc, vc_blk, bias):
    B, L, _ = q_rot.shape
    Lc = kc.shape[1]
    nq = Q_ROWS * GRID_W
    ni = L // nq
    return pl.pallas_call(
        _nbr_attn_kernel,
        grid=(NA_WIDTH // ATT_W, ni, B),
        in_specs=[pl.BlockSpec((1, nq, ATT_W), lambda h, i, b: (b, i, h)),
                  pl.BlockSpec((1, nq, ATT_W), lambda h, i, b: (b, i, h)),
                  pl.BlockSpec((1, L, ATT_W), lambda h, i, b: (b, 0, h)),
                  pl.BlockSpec((1, L, ATT_W), lambda h, i, b: (b, 0, v_blk + h)),
                  pl.BlockSpec((1, Lc, ATT_W), lambda h, i, b: (b, 0, h)),
                  pl.BlockSpec((1, Lc, ATT_W), lambda h, i, b: (b, 0, vc_blk + h)),
                  pl.BlockSpec((ATT_W // HEAD_DIM, 1, nq, K_ROWS * GRID_W), lambda h, i, b: (h, i, 0, 0))],
        out_specs=pl.BlockSpec((1, nq, ATT_W), lambda h, i, b: (b, i, h)),
        out_shape=jax.ShapeDtypeStruct((B, L, NA_WIDTH), BF16),
        compiler_params=_cparams(("parallel", "parallel", "arbitrary"), 48),
        name="nbr_attention",
    )(q_rot, q_plain, k_rot, p, kc, pc, bias)


def _ctx_attn_kernel(q_ref, k_ref, v_ref, o_ref):
    q = q_ref[0].astype(F32)
    k = k_ref[0]
    v = v_ref[0].astype(BF16)
    lane = lax.broadcasted_iota(jnp.int32, q.shape, 1)
    out = None
    for hh in range(ATT_W // HEAD_DIM):
        o = _softmax_pv([_qkt(_head_lanes(q, lane, hh), k)], [v])
        out = o if out is None else jnp.where((lane // HEAD_DIM) == hh, o, out)
    o_ref[0] = out.astype(o_ref.dtype)


def _ctx_attention(qc, kc, pc, vc_blk):
    B, Lc, _ = qc.shape
    blk = lambda off: pl.BlockSpec((1, Lc, ATT_W), lambda h, b: (b, 0, off + h))
    return pl.pallas_call(
        _ctx_attn_kernel,
        grid=(NA_WIDTH // ATT_W, B),
        in_specs=[blk(0), blk(0), blk(vc_blk)],
        out_specs=blk(0),
        out_shape=jax.ShapeDtypeStruct((B, Lc, NA_WIDTH), BF16),
        compiler_params=_cparams(("parallel", "parallel"), 32),
        name="ctx_attention",
    )(qc, kc, pc)


def _nbr_bias(rpb, L):
    R = L // GRID_W
    kh = min(WIN_H, R)
    qc = jnp.arange(GRID_W)[:, None]
    kcol = jnp.arange(GRID_W)[None, :]
    wstart = jnp.clip(qc - WIN_W // 2, 0, GRID_W - WIN_W)
    col_ok = (kcol >= wstart) & (kcol < wstart + WIN_W)
    ext = GRID_W - WIN_W
    rpb_ext = jnp.pad(rpb.astype(F32), ((0, 0), (0, 0), (ext, ext)), mode="edge")
    band = jnp.stack([rpb_ext[:, :, GRID_W - 1 - c:2 * GRID_W - 1 - c] for c in range(GRID_W)], axis=2)
    t = jnp.where(col_ok[None, None], band, NEG)
    t_cat = t.transpose(0, 2, 1, 3).reshape(NA_HEADS, GRID_W, (2 * WIN_H - 1) * GRID_W)
    rows = []
    for r in range(R):
        k0 = min(max(Q_ROWS * (r // Q_ROWS) - WIN_H // 2, 0), R - K_ROWS)
        rstart = min(max(r - kh // 2, 0), R - kh)
        a_lo = rstart - r + (WIN_H - 1)
        win = t_cat[:, :, a_lo * GRID_W:(a_lo + kh) * GRID_W]
        left = (rstart - k0) * GRID_W
        right = (K_ROWS - kh) * GRID_W - left
        rows.append(jnp.pad(win, ((0, 0), (0, 0), (left, right)), constant_values=NEG))
    return jnp.stack(rows, axis=1).reshape(NA_HEADS, R // Q_ROWS, Q_ROWS * GRID_W, K_ROWS * GRID_W)


def _short_conv(u, w, b):
    n = u.shape[0]
    row = lax.broadcasted_iota(jnp.int32, u.shape, 0)
    prev = jnp.where(row == 0, 0.0, pltpu.roll(u, 1, 0))
    nxt = jnp.where(row == n - 1, 0.0, pltpu.roll(u, n - 1, 0))
    return prev * w[0:1] + u * w[1:2] + nxt * w[2:3] + b


def _dft_mats(L):
    n = 2 * L
    f = jnp.arange(L, dtype=jnp.int32)[:, None]
    s = jnp.arange(L, dtype=jnp.int32)[None, :]
    ang = ((f * s) % n).astype(F32) * (2.0 * math.pi / n)
    alt = jnp.where(s % 2 == 0, 1.0, -1.0).astype(F32)
    c = jnp.cos(ang)
    sm = jnp.where(f == 0, alt, -jnp.sin(ang))
    return c.astype(BF16), sm.astype(BF16), sm.T.astype(BF16)


def _filter_features(L):
    pos = jnp.arange(L, dtype=F32)
    t = pos / max(L - 1, 1)
    w = 2.0 * math.pi * pos / L
    f = jnp.linspace(1e-4, HY_BANDS - 1, HY_BANDS, dtype=F32)
    z = jnp.concatenate([t[:, None], jnp.cos(f[None, :] * w[:, None]), -jnp.sin(f[None, :] * w[:, None])], axis=-1)
    return jnp.pad(z, ((0, 0), (0, LANES - z.shape[1])))


def _filter_kernel(z_ref, w1_ref, b1_ref, w2_ref, b2_ref, w3f_ref, w3b_ref, dl_ref, c_ref, s_ref, o_ref):
    L = z_ref.shape[0]
    hid = jnp.sin(HY_SIN_FREQ * (_dot3(z_ref[...], w1_ref[...]) + b1_ref[...]))
    hid = jnp.sin(HY_SIN_FREQ * (_dot3(hid, w2_ref[...]) + b2_ref[...]))
    hf = _dot3(hid, w3f_ref[...])
    hb = _dot3(hid, w3b_ref[...])
    row = lax.broadcasted_iota(jnp.int32, hf.shape, 0)
    t = row.astype(F32) / float(max(L - 1, 1))
    dec = jnp.exp(-t * dl_ref[...])
    hf = hf * dec
    hb = jnp.where(row == 0, 0.0, hb * dec)
    inv = 1.0 / (jnp.sum(jnp.abs(hf), axis=0, keepdims=True) + jnp.sum(jnp.abs(hb), axis=0, keepdims=True) + EPS)
    h = L // 2
    tc = hf.shape[1]
    hf = hf * inv
    hb = hb * inv
    rowh = lax.broadcasted_iota(jnp.int32, (h, tc), 0)
    f0 = hf[:h]
    x = jnp.concatenate([f0, jnp.where(rowh == 0, 0.0, f0), hf[h:], hb[:h], hb[h:]], axis=1).astype(BF16)
    re = _bdot(c_ref[...], x)
    im = _bdot(s_ref[...], x)
    rf0, rf0z, rf1, rb0, rb1 = [re[:, i * tc:(i + 1) * tc] for i in range(5)]
    if0, if0z, if1, ib0, ib1 = [im[:, i * tc:(i + 1) * tc] for i in range(5)]
    sgn = jnp.where(rowh % 2 == 0, 1.0, -1.0)
    conj = lambda i: jnp.where(rowh == 0, i, -i)
    blocks = [(rf0 + rb0, if0 + conj(ib0)),
              (rf1 + sgn * rf0z, if1 + sgn * if0z),
              (rb1 + sgn * rb0, conj(ib1) + sgn * conj(ib0))]
    alpha = jnp.where(rowh == 0, 1.0 / L, 2.0 / L)
    for i, (kr, ki) in enumerate(blocks):
        o_ref[3 * i] = kr * alpha
        o_ref[3 * i + 1] = jnp.where(rowh == 0, 0.0, ki) * alpha
        o_ref[3 * i + 2] = jnp.where(rowh == 0, ki, kr) * alpha


def _pad2(a, r, c):
    return jnp.pad(a, ((0, r - a.shape[0]), (0, c - a.shape[1])))


def _hyena_filter(L, w1, b1, w2, b2, w3, dft):
    c, s, _ = dft
    n = HY_ORDER * HY_WIDTH
    tc = 256
    z = _filter_features(L)
    w1p = _pad2(w1, LANES, LANES)
    w2p = _pad2(w2, LANES, LANES)
    w3p = _pad2(w3, LANES, 2 * n)
    b1p = _pad2(b1[None, :], 1, LANES)
    b2p = _pad2(b2[None, :], 1, LANES)
    deltas = jnp.abs(jnp.linspace(HY_MIN_DECAY, HY_MAX_DECAY, HY_WIDTH, dtype=F32)).reshape(1, HY_WIDTH)
    full = lambda shape: pl.BlockSpec(shape, lambda j: (0, 0))
    const = lambda shape: pl.BlockSpec(shape, lambda j: (0, 0), pipeline_mode=pl.Buffered(1))
    h = L // 2
    return pl.pallas_call(
        _filter_kernel,
        grid=(n // tc,),
        in_specs=[full((L, LANES)), full((LANES, LANES)), full((1, LANES)), full((LANES, LANES)), full((1, LANES)),
                  pl.BlockSpec((LANES, tc), lambda j: (0, j)),
                  pl.BlockSpec((LANES, tc), lambda j: (0, n // tc + j)),
                  pl.BlockSpec((1, tc), lambda j: (0, j % (HY_WIDTH // tc))),
                  const((h, h)), const((h, h))],
        out_specs=pl.BlockSpec((9, h, tc), lambda j: (0, 0, j)),
        out_shape=jax.ShapeDtypeStruct((9, h, n), F32),
        compiler_params=_cparams(("parallel",), 56),
        name="hyena_filter",
    )(z, w1p, b1p, w2p, b2p, w3p, w3p, deltas, c, s)


def _hyena_order_kernel(z_ref, g_ref, swz_ref, sbz_ref, swg_ref, sbg_ref, sk_ref, c_ref, s_ref, st_ref,
                        f_ref, o_ref, *, nchunk, conv_z):
    z = z_ref[0].astype(F32)
    if conv_z:
        z = _short_conv(z, swz_ref[...], sbz_ref[...])
    gate = _short_conv(g_ref[0].astype(F32), swg_ref[...], sbg_ref[...])
    L, tc = z.shape
    h = L // 2
    zb = z.astype(BF16)
    zcat = jnp.concatenate([zb[:h], zb[h:]], axis=1)
    fc = h // nchunk
    conv = None
    for ci in range(nchunk):
        sl = pl.ds(ci * fc, fc)
        zr = _bdot(c_ref[sl, :], zcat)
        zi = _bdot(s_ref[sl, :], zcat)
        ztr, zbr, zti, zbi = zr[:, :tc], zr[:, tc:], zi[:, :tc], zi[:, tc:]
        p0, q0, p20, p1, q1, p21, pm, qm, p2m = [f_ref[i, sl, :] for i in range(9)]
        ytr = ztr * p0 - zti * q0 + zbr * pm - zbi * qm
        yti = ztr * q0 + zti * p20 + zbr * qm + zbi * p2m
        ybr = ztr * p1 - zti * q1 + zbr * p0 - zbi * q0
        ybi = ztr * q1 + zti * p21 + zbr * q0 + zbi * p20
        yr = jnp.concatenate([ytr, ybr], axis=1).astype(BF16)
        yi = jnp.concatenate([yti, ybi], axis=1).astype(BF16)
        part = _bdot(c_ref[:, sl], yr) + _bdot(st_ref[:, sl], yi)
        conv = part if conv is None else conv + part
    conv = jnp.concatenate([conv[:, :tc], conv[:, tc:]], axis=0)
    o_ref[0] = (gate * (conv + z * sk_ref[...])).astype(o_ref.dtype)


def _hyena_order(zsrc, zblk, conv_z, p, hy_blk, part_z, part_g, short_w, short_b, skip, filt, order, dft, out_dtype):
    Bx, Lx, _ = zsrc.shape
    c, s, st = dft
    tc = 256
    nct = HY_WIDTH // tc
    taps = short_w.shape[0]
    h = Lx // 2
    const = lambda shape: pl.BlockSpec(shape, lambda j, b: (0, 0), pipeline_mode=pl.Buffered(1))
    fspec = pl.BlockSpec((9, h, tc), lambda j, b: (0, 0, order * nct + j), pipeline_mode=pl.Buffered(1))
    wspec = lambda part: pl.BlockSpec((taps, tc), lambda j, b: (0, part * nct + j))
    bspec = lambda part: pl.BlockSpec((1, tc), lambda j, b: (0, part * nct + j))
    return pl.pallas_call(
        functools.partial(_hyena_order_kernel, nchunk=max(1, h // 512), conv_z=conv_z),
        grid=(nct, Bx),
        in_specs=[pl.BlockSpec((1, Lx, tc), lambda j, b: (b, 0, zblk + j)),
                  pl.BlockSpec((1, Lx, tc), lambda j, b: (b, 0, hy_blk + part_g * nct + j)),
                  wspec(part_z), bspec(part_z), wspec(part_g), bspec(part_g),
                  pl.BlockSpec((1, tc), lambda j, b: (0, j)),
                  const((h, h)), const((h, h)), const((h, h)),
                  fspec],
        out_specs=pl.BlockSpec((1, Lx, tc), lambda j, b: (b, 0, j)),
        out_shape=jax.ShapeDtypeStruct((Bx, Lx, HY_WIDTH), out_dtype),
        compiler_params=_cparams(("parallel", "arbitrary"), 48),
        name="hyena_long_conv",
    )(zsrc, p, short_w, short_b.reshape(1, -1), short_w, short_b.reshape(1, -1), skip.reshape(1, HY_WIDTH),
      c, s, st, filt)


def _hyena(p, hy_blk, short_w, short_b, filt, skip, dft):
    z1 = _hyena_order(p, hy_blk, True, p, hy_blk, 0, 1, short_w, short_b, skip[0], filt, 0, dft, F32)
    return _hyena_order(z1, 0, False, p, hy_blk, 0, 2, short_w, short_b, skip[1], filt, 1, dft, BF16)


def _merge_kernel(a_ref, h_ref, ga_ref, gb_ref, x_ref, g1_ref, wa_ref, wb_ref, wo_ref, o_ref):
    ga = _sigmoid(ga_ref[0].astype(F32))
    gb = _sigmoid(gb_ref[0].astype(F32))
    y = ga * _bdot(a_ref[0], wa_ref[...]) + gb * _bdot(h_ref[0], wb_ref[...])
    o_ref[0] = x_ref[0] + g1_ref[0] * _bdot(y.astype(BF16), wo_ref[...])


def _merge(attn, hyena, p, gate_blk, x, g1, wa, wb, wo):
    Bx, Lx, D = x.shape
    tm = min(Lx, 512)
    W = attn.shape[2]
    const = lambda shape: pl.BlockSpec(shape, lambda b, i: (0, 0))
    return pl.pallas_call(
        _merge_kernel,
        grid=(Bx, Lx // tm),
        in_specs=[pl.BlockSpec((1, tm, W), lambda b, i: (b, i, 0)),
                  pl.BlockSpec((1, tm, W), lambda b, i: (b, i, 0)),
                  pl.BlockSpec((1, tm, D), lambda b, i: (b, i, gate_blk)),
                  pl.BlockSpec((1, tm, D), lambda b, i: (b, i, gate_blk + 1)),
                  pl.BlockSpec((1, tm, D), lambda b, i: (b, i, 0)),
                  pl.BlockSpec((1, 1, D), lambda b, i: (b, 0, 0)),
                  const((W, D)), const((W, D)), const((D, D))],
        out_specs=pl.BlockSpec((1, tm, D), lambda b, i: (b, i, 0)),
        out_shape=jax.ShapeDtypeStruct((Bx, Lx, D), F32),
        compiler_params=_cparams(("parallel", "parallel"), 48),
        name="merge_out_proj",
    )(attn, hyena, p, p, x, g1, wa, wb, wo)


MOE_TILE = 896
MOE_CHUNK = 256
GROUP_LANE = N_EXPERTS
RANK_LANE = N_EXPERTS + 1


def _tile_rows(tm, n_tokens, seg):
    start = pl.program_id(0) * tm
    row = lax.broadcasted_iota(jnp.int32, (tm, 1), 0) + start
    return row < n_tokens, row < (start // seg + 1) * seg


def _router_kernel(x_ref, g_ref, sca_ref, sha_ref, scb_ref, shb_ref, wr_ref, br_ref, ht_ref, cw_ref, cwt_ref,
                   *, n_tokens, seg):
    valid, first = _tile_rows(x_ref.shape[1], n_tokens, seg)
    x = jnp.where(valid, x_ref[0], 0.0)
    sc = jnp.where(first, sca_ref[0], scb_ref[0])
    sh = jnp.where(first, sha_ref[0], shb_ref[0])
    h = jnp.where(valid, _rms_mod(x, g_ref[...], sc, sh), 0.0)
    ht_ref[0] = h.T.astype(BF16)
    logits = _dot3(h, wr_ref[...]) + br_ref[...]
    lane = lax.broadcasted_iota(jnp.int32, logits.shape, 1)
    lane_f = lane.astype(F32)
    big = float(LANES)
    is_g = (lane >= N_EXPERTS) & (lane < N_EXPERTS + N_GROUPS)
    gl = jnp.where(is_g, logits, NEG)
    gmax = gl.max(axis=-1, keepdims=True)
    gp = 1.0 / jnp.where(is_g, jnp.exp(gl - gmax), 0.0).sum(axis=-1, keepdims=True)
    gidx = jnp.where(is_g & (gl == gmax), lane_f - N_EXPERTS, big).min(axis=-1, keepdims=True)
    gidx = jnp.where(valid, gidx, -1.0)
    in_grp =(lane < N_EXPERTS) & ((lane // EXPERTS_PER_GROUP).astype(F32) == gidx)
    el = jnp.where(in_grp, logits, NEG)
    v1 = el.max(axis=-1, keepdims=True)
    i1 = jnp.where(in_grp & (el == v1), lane_f, big).min(axis=-1, keepdims=True)
    rest = in_grp & (lane_f != i1)
    el2 = jnp.where(rest, logits, NEG)
    v2 = el2.max(axis=-1, keepdims=True)
    i2 = jnp.where(rest & (el2 == v2), lane_f, big).min(axis=-1, keepdims=True)
    e2 = jnp.exp(v2 - v1)
    w1 = gp / (1.0 + e2)
    cw = jnp.where(lane_f == i1, w1, jnp.where(lane_f == i2, w1 * e2, 0.0))
    tm = h.shape[0]
    onehot = jnp.where((lane_f == gidx) & (lane < N_GROUPS), 1.0, 0.0)
    tri = lax.broadcasted_iota(jnp.int32, (tm, tm), 1) <= lax.broadcasted_iota(jnp.int32, (tm, tm), 0)
    cum = _bdot(jnp.where(tri, 1.0, 0.0).astype(BF16), onehot.astype(BF16))
    rank = (onehot * cum).sum(axis=-1, keepdims=True) - 1.0
    rec = jnp.where(lane == GROUP_LANE, gidx, jnp.where(lane == RANK_LANE, rank, cw))
    cw_ref[0] = rec
    cwt_ref[0] = rec.T


def _seg_spec(tm, seg, nseg, D, off):
    return pl.BlockSpec((1, 1, D), lambda i, *_: (jnp.minimum((i * tm) // seg + off, nseg - 1), 0, 0))


def _router(x, g, sc, sh, w_group, b_group, w_router, b_router):
    nseg, seg, D = x.shape
    n_tokens = nseg * seg
    tm = MOE_TILE
    nt = pl.cdiv(n_tokens, tm)
    wr = _pad2(jnp.concatenate([w_router, w_group], axis=1), D, LANES)
    br = _pad2(jnp.concatenate([b_router, b_group])[None, :], 1, LANES)
    segv = lambda off: _seg_spec(tm, seg, nseg, D, off)
    return pl.pallas_call(
        functools.partial(_router_kernel, n_tokens=n_tokens, seg=seg),
        grid=(nt,),
        in_specs=[pl.BlockSpec((1, tm, D), lambda i: (0, i, 0)),
                  pl.BlockSpec((1, D), lambda i: (0, 0)),
                  segv(0), segv(0), segv(1), segv(1),
                  pl.BlockSpec((D, LANES), lambda i: (0, 0)),
                  pl.BlockSpec((1, LANES), lambda i: (0, 0))],
        out_specs=[pl.BlockSpec((1, D, tm), lambda i: (0, 0, i)),
                   pl.BlockSpec((1, tm, LANES), lambda i: (0, i, 0)),
                   pl.BlockSpec((1, LANES, tm), lambda i: (0, 0, i))],
        out_shape=[jax.ShapeDtypeStruct((1, D, nt * tm), BF16), jax.ShapeDtypeStruct((1, nt * tm, LANES), F32),
                   jax.ShapeDtypeStruct((1, LANES, nt * tm), F32)],
        compiler_params=_cparams(("parallel",), 48),
        name="moe_router",
    )(x.reshape(1, n_tokens, D), g.reshape(1, D), sc, sh, sc, sh, wr, br)


def _moe_kernel(cnt_ref, ht_ref, cw_ref, cwt_ref, x_ref, g2a_ref, g2b_ref, w1t_ref, w3t_ref, w2t_ref, o_ref, acc_ref,
                *, n_tokens, seg):
    i, g = pl.program_id(0), pl.program_id(1)

    @pl.when(g == 0)
    def _():
        acc_ref[...] = jnp.zeros_like(acc_ref)

    n = cnt_ref[i * N_GROUPS + g]
    gf = g.astype(F32)
    tm = acc_ref.shape[1]
    c = MOE_CHUNK
    sel_col = jnp.where(cw_ref[0, :, GROUP_LANE:GROUP_LANE + 1] == gf, cw_ref[0, :, RANK_LANE:RANK_LANE + 1], -1.0)
    sel_row = jnp.where(cwt_ref[0, GROUP_LANE:GROUP_LANE + 1, :] == gf, cwt_ref[0, RANK_LANE:RANK_LANE + 1, :], -1.0)
    e0 = pl.multiple_of(g * EXPERTS_PER_GROUP, EXPERTS_PER_GROUP)
    cw_hi, cw_lo = _split(cwt_ref[0, pl.ds(e0, EXPERTS_PER_GROUP), :])
    lane_r = lax.broadcasted_iota(jnp.int32, (tm, c), 1).astype(F32)
    sub_r = lax.broadcasted_iota(jnp.int32, (c, tm), 0).astype(F32)

    def chunk(k, carry):
        r0 = (k * c).astype(F32)
        pkt = jnp.where(sel_col - r0 == lane_r, 1.0, 0.0).astype(BF16)
        pk = jnp.where(sel_row - r0 == sub_r, 1.0, 0.0).astype(BF16)
        xst = _bdot(ht_ref[0], pkt).astype(BF16)
        cws = _bdot(cw_hi, pkt) + _bdot(cw_lo, pkt)
        at = _bdot(w1t_ref[0], xst)
        bt = _bdot(w3t_ref[0], xst)
        out = None
        for e in range(EXPERTS_PER_GROUP):
            rows = slice(e * D_EXPERT, (e + 1) * D_EXPERT)
            a = at[rows]
            hid = (a * _sigmoid(a)) * bt[rows] * cws[e:e + 1]
            part = _bdot(w2t_ref[0, :, rows], hid.astype(BF16))
            out = part if out is None else out + part
        acc_ref[...] += _bdot(out.astype(BF16), pk)
        return carry

    lax.fori_loop(0, (n + c - 1) // c, chunk, 0)

    @pl.when(g == N_GROUPS - 1)
    def _():
        _, first = _tile_rows(tm, n_tokens, seg)
        o_ref[0] = x_ref[0] + jnp.where(first, g2a_ref[0], g2b_ref[0]) * acc_ref[...].T


def _moe(ht, cw, cwt, x, g2, w1t, w3t, w2t):
    nseg, seg, D = x.shape
    n_tokens = nseg * seg
    tm = MOE_TILE
    nt = cw.shape[1] // tm
    ef = EXPERTS_PER_GROUP * D_EXPERT
    gid = cw[0, :, GROUP_LANE].reshape(nt, tm, 1)
    counts = jnp.sum(gid == jnp.arange(N_GROUPS, dtype=F32), axis=1).astype(jnp.int32).reshape(-1)
    once = pl.Buffered(1)
    segv = lambda off: _seg_spec(tm, seg, nseg, D, off)
    grid_spec = pltpu.PrefetchScalarGridSpec(
        num_scalar_prefetch=1,
        grid=(nt, N_GROUPS),
        in_specs=[pl.BlockSpec((1, D, tm), lambda i, g, cnt: (0, 0, i), pipeline_mode=once),
                  pl.BlockSpec((1, tm, LANES), lambda i, g, cnt: (0, i, 0)),
                  pl.BlockSpec((1, LANES, tm), lambda i, g, cnt: (0, 0, i)),
                  pl.BlockSpec((1, tm, D), lambda i, g, cnt: (0, i, 0), pipeline_mode=once),
                  segv(0), segv(1),
                  pl.BlockSpec((1, ef, D), lambda i, g, cnt: (g, 0, 0)),
                  pl.BlockSpec((1, ef, D), lambda i, g, cnt: (g, 0, 0)),
                  pl.BlockSpec((1, D, ef), lambda i, g, cnt: (g, 0, 0))],
        out_specs=pl.BlockSpec((1, tm, D), lambda i, g, cnt: (0, i, 0)),
        scratch_shapes=[pltpu.VMEM((D, tm), F32)])
    out = pl.pallas_call(
        functools.partial(_moe_kernel, n_tokens=n_tokens, seg=seg),
        grid_spec=grid_spec,
        out_shape=jax.ShapeDtypeStruct((1, n_tokens, D), F32),
        compiler_params=_cparams(("parallel", "arbitrary"), 56),
        name="moe_experts",
    )(counts, ht, cw, cwt, x.reshape(1, n_tokens, D), g2, g2, w1t, w3t, w2t)
    return out.reshape(nseg, seg, D)


def _layer(x, xc, mods, modc, last, lw, consts):
    B, L, D = x.shape
    Lc = xc.shape[1]
    sh1, sc1, g1, sh2, sc2, g2 = mods
    sh1c, sc1c, g1c, sh2c, sc2c, g2c = modc
    w_in = lw["w_in"]
    hy_blk = 3 * NA_WIDTH // 256
    gate_blk = (3 * NA_WIDTH + (HY_ORDER + 1) * HY_WIDTH) // D
    v_blk = 2 * NA_WIDTH // ATT_W

    p = _norm_mod_matmul(x, lw["norm_mix"], sc1, sh1, w_in)
    q_rot, q_plain, k_rot = _qk_prep(p, 0, 1, lw["q_norm"], lw["k_norm"], consts["rope"])
    if last:
        pc = _norm_mod_matmul(xc, lw["norm_mix"], sc1c, sh1c, w_in[:, NA_WIDTH:3 * NA_WIDTH])
        _, kc = _qk_prep(pc, 0, 0, lw["q_norm"], lw["k_norm"], None)
        vc_blk = NA_WIDTH // ATT_W
    else:
        pc = _norm_mod_matmul(xc, lw["norm_mix"], sc1c, sh1c, w_in)
        qc, kc = _qk_prep(pc, 0, 1, lw["q_norm"], lw["k_norm"], None)
        vc_blk = v_blk
    attn = _nbr_attention(q_rot, q_plain, k_rot, p, v_blk, kc, pc, vc_blk, _nbr_bias(lw["rpb"], L))
    flt = (lw["flt_w1"], lw["flt_b1"], lw["flt_w2"], lw["flt_b2"], lw["flt_w3"])
    filt = _hyena_filter(L, *flt, consts["dft"])
    hyena = _hyena(p, hy_blk, lw["short_w"], lw["short_b"], filt, lw["hy_skip"], consts["dft"])
    x = _merge(attn, hyena, p, gate_blk, x, g1, lw["w_br_a"], lw["w_br_b"], lw["w_out"])
    rw = (lw["w_group"], lw["b_group"], lw["w_router"], lw["b_router"])
    ew = (lw["moe_w1t"], lw["moe_w3t"], lw["moe_w2t"])
    x = _moe(*_router(x, lw["norm_ffn"], sc2, sh2, *rw), x, g2, *ew)
    if last:
        return x, xc

    attn_c = _ctx_attention(qc, kc, pc, vc_blk)
    filt_c = _hyena_filter(Lc, *flt, consts["dft_c"])
    hyena_c = _hyena(pc, hy_blk, lw["short_w"], lw["short_b"], filt_c, lw["hy_skip"], consts["dft_c"])
    xc = _merge(attn_c, hyena_c, pc, gate_blk, xc, g1c, lw["w_br_a"], lw["w_br_b"], lw["w_out"])
    xc = _moe(*_router(xc, lw["norm_ffn"], sc2c, sh2c, *rw), xc, g2c, *ew)
    return x, xc


def kernel(x, c, ctx, c_ctx, ada_w, ada_b, norm_mix, norm_ffn, w_in, q_norm, k_norm, rpb, short_w, short_b, flt_w1, flt_b1, flt_w2, flt_b2, flt_w3, hy_skip, w_br_a, w_br_b, w_out, w_group, b_group, w_router, b_router, moe_w1, moe_w3, moe_w2):
    B, L, D = x.shape
    Lc = ctx.shape[1]
    depth = ada_w.shape[0]
    consts = {"rope": _rope_tables(L), "dft": _dft_mats(L // 2), "dft_c": _dft_mats(Lc // 2)}
    rows = 8 * ((B + 1 + 7) // 8)
    cs = jnp.pad(jnp.concatenate([c, c_ctx[None, :]], axis=0), ((0, rows - B - 1), (0, 0)))
    xc = ctx
    for i in range(depth):
        mod = _ada(cs, ada_w[i], ada_b[i])
        mods = [m.reshape(B, 1, D) for m in jnp.split(mod[:B], 6, axis=-1)]
        modc = [jnp.broadcast_to(m.reshape(1, 1, D), (B, 1, D)) for m in jnp.split(mod[B], 6, axis=-1)]
        ef = EXPERTS_PER_GROUP * D_EXPERT
        lw = {
            "norm_mix": norm_mix[i], "norm_ffn": norm_ffn[i], "w_in": w_in[i].astype(BF16),
            "q_norm": q_norm[i], "k_norm": k_norm[i], "rpb": rpb[i],
            "short_w": short_w[i], "short_b": short_b[i],
            "flt_w1": flt_w1[i], "flt_b1": flt_b1[i], "flt_w2": flt_w2[i], "flt_b2": flt_b2[i], "flt_w3": flt_w3[i],
            "hy_skip": hy_skip[i],
            "w_br_a": w_br_a[i].astype(BF16), "w_br_b": w_br_b[i].astype(BF16), "w_out": w_out[i].astype(BF16),
            "w_group": w_group[i], "b_group": b_group[i], "w_router": w_router[i], "b_router": b_router[i],
            "moe_w1t": moe_w1[i].astype(BF16).transpose(0, 1, 3, 2).reshape(N_GROUPS, ef, D),
            "moe_w3t": moe_w3[i].astype(BF16).transpose(0, 1, 3, 2).reshape(N_GROUPS, ef, D),
            "moe_w2t": moe_w2[i].astype(BF16).transpose(0, 3, 1, 2).reshape(N_GROUPS, D, ef),
        }
        x, xc = _layer(x, xc, mods, modc, i == depth - 1, lw, consts)
    return x
```

```python
import functools
import math

import jax
import jax.numpy as jnp
from jax import lax
from jax.experimental import pallas as pl
from jax.experimental.pallas import tpu as pltpu

F32 = jnp.float32
BF16 = jnp.bfloat16

D_MODEL = 1024
GRID_W = 64
NA_HEADS = 8
HEAD_DIM = 64
NA_WIDTH = NA_HEADS * HEAD_DIM
WIN_H = 8
WIN_W = 16
ROPE_THETA = 100.0
HY_WIDTH = 512
HY_ORDER = 2
HY_BANDS = 16
HY_SIN_FREQ = 1.0
HY_MAX_DECAY = math.log(1e-2) / 0.3
HY_MIN_DECAY = math.log(1e-2) / 1.5
N_GROUPS = 4
EXPERTS_PER_GROUP = 8
N_EXPERTS = N_GROUPS * EXPERTS_PER_GROUP
D_EXPERT = 256
EPS = 1e-6
NEG = -1e30

LANES = 128
V7X_VMEM_BYTES = 64 * 1024 * 1024
ATT_W = 256
Q_ROWS = 4
K_ROWS = Q_ROWS + WIN_H


def _cparams(sem, vmem_mb):
    assert vmem_mb * 1024 * 1024 < V7X_VMEM_BYTES
    return pltpu.CompilerParams(dimension_semantics=sem, vmem_limit_bytes=vmem_mb * 1024 * 1024)


def _bdot(a, b):
    return jnp.dot(a, b, preferred_element_type=F32)


def _split(a):
    hi = a.astype(BF16)
    lo = (a - hi.astype(F32)).astype(BF16)
    return hi, lo


def _dot3(a, b):
    ah, al = _split(a)
    bh, bl = _split(b)
    return _bdot(ah, bh) + _bdot(ah, bl) + _bdot(al, bh)


def _sigmoid(x):
    return 1.0 / (1.0 + jnp.exp(-x))


def _rms_mod(x, g, sc, sh):
    ms = jnp.mean(x * x, axis=-1, keepdims=True)
    return (x * lax.rsqrt(ms + EPS) * g) * (1.0 + sc) + sh


def _ada_kernel(c_ref, w_ref, b_ref, o_ref):
    c = c_ref[...]
    o_ref[...] = _dot3(c * _sigmoid(c), w_ref[...]) + b_ref[...]


def _ada(cs, w, b):
    R, D = cs.shape
    N = w.shape[1]
    tn = 512
    return pl.pallas_call(
        _ada_kernel,
        grid=(N // tn,),
        in_specs=[pl.BlockSpec((R, D), lambda j: (0, 0)),
                  pl.BlockSpec((D, tn), lambda j: (0, j)),
                  pl.BlockSpec((1, tn), lambda j: (0, j))],
        out_specs=pl.BlockSpec((R, tn), lambda j: (0, j)),
        out_shape=jax.ShapeDtypeStruct((R, N), F32),
        compiler_params=_cparams(("parallel",), 32),
        name="ada_mod",
    )(cs, w, b.reshape(1, N))


def _nmm_kernel(x_ref, g_ref, sc_ref, sh_ref, w_ref, o_ref, h_ref):
    @pl.when(pl.program_id(2) == 0)
    def _():
        h_ref[...] = _rms_mod(x_ref[0], g_ref[...], sc_ref[0], sh_ref[0]).astype(BF16)

    o_ref[0] = _bdot(h_ref[...], w_ref[...]).astype(o_ref.dtype)


def _norm_mod_matmul(x, g, sc, sh, w):
    Bx, Lx, D = x.shape
    N = w.shape[1]
    tm = min(Lx, 1024)
    tn = 1280 if N % 1280 == 0 else 1024
    return pl.pallas_call(
        _nmm_kernel,
        grid=(Bx, Lx // tm, N // tn),
        in_specs=[pl.BlockSpec((1, tm, D), lambda b, i, j: (b, i, 0)),
                  pl.BlockSpec((1, D), lambda b, i, j: (0, 0)),
                  pl.BlockSpec((1, 1, D), lambda b, i, j: (b, 0, 0)),
                  pl.BlockSpec((1, 1, D), lambda b, i, j: (b, 0, 0)),
                  pl.BlockSpec((D, tn), lambda b, i, j: (0, j))],
        out_specs=pl.BlockSpec((1, tm, tn), lambda b, i, j: (b, i, j)),
        out_shape=jax.ShapeDtypeStruct((Bx, Lx, N), BF16),
        scratch_shapes=[pltpu.VMEM((tm, D), BF16)],
        compiler_params=_cparams(("parallel", "parallel", "arbitrary"), 40),
        name="norm_mod_proj",
    )(x, g.reshape(1, D), sc, sh, w)


def _head_norm(x, gn, bd):
    hi, lo = _split(x * x)
    ms = _bdot(hi, bd) + _bdot(lo, bd)
    return x * lax.rsqrt(ms + EPS) * gn


def _rope(x, cos, sin_signed):
    lane = lax.broadcasted_iota(jnp.int32, x.shape, 1)
    quarter = HEAD_DIM // 4
    partner = jnp.where((lane % (2 * quarter)) < quarter,
                        pltpu.roll(x, x.shape[1] - quarter, 1), pltpu.roll(x, quarter, 1))
    return x * cos + partner * sin_signed


def _qk_rope_kernel(q_ref, k_ref, qn_ref, kn_ref, cos_ref, sin_ref, bd_ref, qr_ref, qp_ref, kr_ref):
    bd = bd_ref[...]
    cos = cos_ref[...]
    sin = sin_ref[...]
    q = _head_norm(q_ref[0].astype(F32), qn_ref[...], bd) * (HEAD_DIM ** -0.5)
    k = _head_norm(k_ref[0].astype(F32), kn_ref[...], bd)
    qp_ref[0] = q.astype(BF16)
    qr_ref[0] = _rope(q, cos, sin).astype(BF16)
    kr_ref[0] = _rope(k, cos, sin).astype(BF16)


def _qk_plain_kernel(q_ref, k_ref, qn_ref, kn_ref, bd_ref, qp_ref, kp_ref):
    bd = bd_ref[...]
    qp_ref[0] = (_head_norm(q_ref[0].astype(F32), qn_ref[...], bd) * (HEAD_DIM ** -0.5)).astype(BF16)
    kp_ref[0] = _head_norm(k_ref[0].astype(F32), kn_ref[...], bd).astype(BF16)


def _head_block_diag():
    r = jnp.arange(NA_WIDTH) // HEAD_DIM
    return jnp.where(r[:, None] == r[None, :], 1.0 / HEAD_DIM, 0.0).astype(BF16)


def _qk_prep(p, qcol, kcol, qn, kn, rope_tabs):
    Bx, Lx, _ = p.shape
    W = NA_WIDTH
    tm = min(Lx, 512)
    tok = lambda c: pl.BlockSpec((1, tm, W), lambda b, i: (b, i, c))
    vec = pl.BlockSpec((1, W), lambda b, i: (0, 0))
    mat = pl.BlockSpec((W, W), lambda b, i: (0, 0))
    out = pl.BlockSpec((1, tm, W), lambda b, i: (b, i, 0))
    osd = jax.ShapeDtypeStruct((Bx, Lx, W), BF16)
    qn_t = jnp.tile(qn, NA_HEADS).reshape(1, W)
    kn_t = jnp.tile(kn, NA_HEADS).reshape(1, W)
    bd = _head_block_diag()
    if rope_tabs is None:
        return pl.pallas_call(
            _qk_plain_kernel, grid=(Bx, Lx // tm),
            in_specs=[tok(qcol), tok(kcol), vec, vec, mat],
            out_specs=[out, out], out_shape=[osd, osd],
            compiler_params=_cparams(("parallel", "parallel"), 32), name="qk_norm",
        )(p, p, qn_t, kn_t, bd)
    cos, sin = rope_tabs
    tab = pl.BlockSpec((tm, W), lambda b, i: (i, 0))
    return pl.pallas_call(
        _qk_rope_kernel, grid=(Bx, Lx // tm),
        in_specs=[tok(qcol), tok(kcol), vec, vec, tab, tab, mat],
        out_specs=[out, out, out], out_shape=[osd, osd, osd],
        compiler_params=_cparams(("parallel", "parallel"), 32), name="qk_norm_rope",
    )(p, p, qn_t, kn_t, cos, sin, bd)


def _rope_tables(L):
    quarter = HEAD_DIM // 4
    freqs = ROPE_THETA ** (-jnp.arange(quarter, dtype=F32) / quarter)
    pos = jnp.arange(L)
    rows, cols = (pos // GRID_W).astype(F32), (pos % GRID_W).astype(F32)
    d = jnp.arange(NA_WIDTH) % HEAD_DIM
    p = jnp.where((d < HEAD_DIM // 2)[None, :], rows[:, None], cols[:, None])
    ang = p * freqs[d % quarter][None, :]
    sign = jnp.where((d % (2 * quarter)) < quarter, -1.0, 1.0)[None, :]
    return jnp.cos(ang), jnp.sin(ang) * sign


def _softmax_pv(s_list, v_list):
    m = s_list[0].max(axis=-1, keepdims=True)
    for s in s_list[1:]:
        m = jnp.maximum(m, s.max(axis=-1, keepdims=True))
    l = 0.0
    o = 0.0
    for s, v in zip(s_list, v_list):
        p = jnp.exp(s - m)
        l = l + p.sum(axis=-1, keepdims=True)
        o = o + _bdot(p.astype(BF16), v)
    return o / l


def _qkt(q, k):
    return lax.dot_general(q, k, (((1,), (1,)), ((), ())), preferred_element_type=F32)


def _head_lanes(x, lane, hh):
    return jnp.where((lane // HEAD_DIM) == hh, x, 0.0).astype(BF16)


def _nbr_attn_kernel(q_ref, qp_ref, k_ref, v_ref, kc_ref, vc_ref, bias_ref, o_ref):
    i = pl.program_id(1)
    max_row0 = k_ref.shape[1] // GRID_W - K_ROWS
    k0 = pl.multiple_of(jnp.clip(Q_ROWS * i - WIN_H // 2, 0, max_row0) * GRID_W, 4 * GRID_W)
    nk = K_ROWS * GRID_W
    k = k_ref[0, pl.ds(k0, nk), :]
    v = v_ref[0, pl.ds(k0, nk), :].astype(BF16)
    kc = kc_ref[0]
    vc = vc_ref[0].astype(BF16)
    q = q_ref[0].astype(F32)
    qp = qp_ref[0].astype(F32)
    lane = lax.broadcasted_iota(jnp.int32, q.shape, 1)
    out = None
    for hh in range(ATT_W // HEAD_DIM):
        s_w = _qkt(_head_lanes(q, lane, hh), k) + bias_ref[hh, 0]
        s_c = _qkt(_head_lanes(qp, lane, hh), kc)
        o = _softmax_pv([s_w, s_c], [v, vc])
        out = o if out is None else jnp.where((lane // HEAD_DIM) == hh, o, out)
    o_ref[0] = out.astype(o_ref.dtype)


def _nbr_attention(q_rot, q_plain, k_rot, p, v_blk, kc, pc, vc_blk, bias):
    B, L, _ = q_rot.shape
    Lc = kc.shape[1]
    nq = Q_ROWS * GRID_W
    ni = L // nq
    return pl.pallas_call(
        _nbr_attn_kernel,
        grid=(NA_WIDTH // ATT_W, ni, B),
        in_specs=[pl.BlockSpec((1, nq, ATT_W), lambda h, i, b: (b, i, h)),
                  pl.BlockSpec((1, nq, ATT_W), lambda h, i, b: (b, i, h)),
                  pl.BlockSpec((1, L, ATT_W), lambda h, i, b: (b, 0, h)),
                  pl.BlockSpec((1, L, ATT_W), lambda h, i, b: (b, 0, v_blk + h)),
                  pl.BlockSpec((1, Lc, ATT_W), lambda h, i, b: (b, 0, h)),
                  pl.BlockSpec((1, Lc, ATT_W), lambda h, i, b: (b, 0, vc_blk + h)),
                  pl.BlockSpec((ATT_W // HEAD_DIM, 1, nq, K_ROWS * GRID_W), lambda h, i, b: (h, i, 0, 0))],
        out_specs=pl.BlockSpec((1, nq, ATT_W), lambda h, i, b: (b, i, h)),
        out_shape=jax.ShapeDtypeStruct((B, L, NA_WIDTH), BF16),
        compiler_params=_cparams(("parallel", "parallel", "arbitrary"), 48),
        name="nbr_attention",
    )(q_rot, q_plain, k_rot, p, kc, pc, bias)


def _ctx_attn_kernel(q_ref, k_ref, v_ref, o_ref):
    q = q_ref[0].astype(F32)
    k = k_ref[0]
    v = v_ref[0].astype(BF16)
    lane = lax.broadcasted_iota(jnp.int32, q.shape, 1)
    out = None
    for hh in range(ATT_W // HEAD_DIM):
        o = _softmax_pv([_qkt(_head_lanes(q, lane, hh), k)], [v])
        out = o if out is None else jnp.where((lane // HEAD_DIM) == hh, o, out)
    o_ref[0] = out.astype(o_ref.dtype)


def _ctx_attention(qc, kc, pc, vc_blk):
    B, Lc, _ = qc.shape
    blk = lambda off: pl.BlockSpec((1, Lc, ATT_W), lambda h, b: (b, 0, off + h))
    return pl.pallas_call(
        _ctx_attn_kernel,
        grid=(NA_WIDTH // ATT_W, B),
        in_specs=[blk(0), blk(0), blk(vc_blk)],
        out_specs=blk(0),
        out_shape=jax.ShapeDtypeStruct((B, Lc, NA_WIDTH), BF16),
        compiler_params=_cparams(("parallel", "parallel"), 32),
        name="ctx_attention",
    )(qc, kc, pc)


def _nbr_bias(rpb, L):
    R = L // GRID_W
    kh = min(WIN_H, R)
    qc = jnp.arange(GRID_W)[:, None]
    kcol = jnp.arange(GRID_W)[None, :]
    wstart = jnp.clip(qc - WIN_W // 2, 0, GRID_W - WIN_W)
    col_ok = (kcol >= wstart) & (kcol < wstart + WIN_W)
    ext = GRID_W - WIN_W
    rpb_ext = jnp.pad(rpb.astype(F32), ((0, 0), (0, 0), (ext, ext)), mode="edge")
    band = jnp.stack([rpb_ext[:, :, GRID_W - 1 - c:2 * GRID_W - 1 - c] for c in range(GRID_W)], axis=2)
    t = jnp.where(col_ok[None, None], band, NEG)
    t_cat = t.transpose(0, 2, 1, 3).reshape(NA_HEADS, GRID_W, (2 * WIN_H - 1) * GRID_W)
    rows = []
    for r in range(R):
        k0 = min(max(Q_ROWS * (r // Q_ROWS) - WIN_H // 2, 0), R - K_ROWS)
        rstart = min(max(r - kh // 2, 0), R - kh)
        a_lo = rstart - r + (WIN_H - 1)
        win = t_cat[:, :, a_lo * GRID_W:(a_lo + kh) * GRID_W]
        left = (rstart - k0) * GRID_W
        right = (K_ROWS - kh) * GRID_W - left
        rows.append(jnp.pad(win, ((0, 0), (0, 0), (left, right)), constant_values=NEG))
    return jnp.stack(rows, axis=1).reshape(NA_HEADS, R // Q_ROWS, Q_ROWS * GRID_W, K_ROWS * GRID_W)


def _short_conv(u, w, b):
    n = u.shape[0]
    row = lax.broadcasted_iota(jnp.int32, u.shape, 0)
    prev = jnp.where(row == 0, 0.0, pltpu.roll(u, 1, 0))
    nxt = jnp.where(row == n - 1, 0.0, pltpu.roll(u, n - 1, 0))
    return prev * w[0:1] + u * w[1:2] + nxt * w[2:3] + b


def _dft_mats(L):
    n = 2 * L
    f = jnp.arange(L, dtype=jnp.int32)[:, None]
    s = jnp.arange(L, dtype=jnp.int32)[None, :]
    ang = ((f * s) % n).astype(F32) * (2.0 * math.pi / n)
    alt = jnp.where(s % 2 == 0, 1.0, -1.0).astype(F32)
    c = jnp.cos(ang)
    sm = jnp.where(f == 0, alt, -jnp.sin(ang))
    return c.astype(BF16), sm.astype(BF16), sm.T.astype(BF16)


def _filter_features(L):
    pos = jnp.arange(L, dtype=F32)
    t = pos / max(L - 1, 1)
    w = 2.0 * math.pi * pos / L
    f = jnp.linspace(1e-4, HY_BANDS - 1, HY_BANDS, dtype=F32)
    z = jnp.concatenate([t[:, None], jnp.cos(f[None, :] * w[:, None]), -jnp.sin(f[None, :] * w[:, None])], axis=-1)
    return jnp.pad(z, ((0, 0), (0, LANES - z.shape[1])))


def _filter_kernel(z_ref, w1_ref, b1_ref, w2_ref, b2_ref, w3f_ref, w3b_ref, dl_ref, c_ref, s_ref, o_ref):
    L = z_ref.shape[0]
    hid = jnp.sin(HY_SIN_FREQ * (_dot3(z_ref[...], w1_ref[...]) + b1_ref[...]))
    hid = jnp.sin(HY_SIN_FREQ * (_dot3(hid, w2_ref[...]) + b2_ref[...]))
    hf = _dot3(hid, w3f_ref[...])
    hb = _dot3(hid, w3b_ref[...])
    row = lax.broadcasted_iota(jnp.int32, hf.shape, 0)
    t = row.astype(F32) / float(max(L - 1, 1))
    dec = jnp.exp(-t * dl_ref[...])
    hf = hf * dec
    hb = jnp.where(row == 0, 0.0, hb * dec)
    inv = 1.0 / (jnp.sum(jnp.abs(hf), axis=0, keepdims=True) + jnp.sum(jnp.abs(hb), axis=0, keepdims=True) + EPS)
    h = L // 2
    tc = hf.shape[1]
    hf = hf * inv
    hb = hb * inv
    rowh = lax.broadcasted_iota(jnp.int32, (h, tc), 0)
    f0 = hf[:h]
    x = jnp.concatenate([f0, jnp.where(rowh == 0, 0.0, f0), hf[h:], hb[:h], hb[h:]], axis=1).astype(BF16)
    re = _bdot(c_ref[...], x)
    im = _bdot(s_ref[...], x)
    rf0, rf0z, rf1, rb0, rb1 = [re[:, i * tc:(i + 1) * tc] for i in range(5)]
    if0, if0z, if1, ib0, ib1 = [im[:, i * tc:(i + 1) * tc] for i in range(5)]
    sgn = jnp.where(rowh % 2 == 0, 1.0, -1.0)
    conj = lambda i: jnp.where(rowh == 0, i, -i)
    blocks = [(rf0 + rb0, if0 + conj(ib0)),
              (rf1 + sgn * rf0z, if1 + sgn * if0z),
              (rb1 + sgn * rb0, conj(ib1) + sgn * conj(ib0))]
    alpha = jnp.where(rowh == 0, 1.0 / L, 2.0 / L)
    for i, (kr, ki) in enumerate(blocks):
        o_ref[3 * i] = kr * alpha
        o_ref[3 * i + 1] = jnp.where(rowh == 0, 0.0, ki) * alpha
        o_ref[3 * i + 2] = jnp.where(rowh == 0, ki, kr) * alpha


def _pad2(a, r, c):
    return jnp.pad(a, ((0, r - a.shape[0]), (0, c - a.shape[1])))


def _hyena_filter(L, w1, b1, w2, b2, w3, dft):
    c, s, _ = dft
    n = HY_ORDER * HY_WIDTH
    tc = 256
    z = _filter_features(L)
    w1p = _pad2(w1, LANES, LANES)
    w2p = _pad2(w2, LANES, LANES)
    w3p = _pad2(w3, LANES, 2 * n)
    b1p = _pad2(b1[None, :], 1, LANES)
    b2p = _pad2(b2[None, :], 1, LANES)
    deltas = jnp.abs(jnp.linspace(HY_MIN_DECAY, HY_MAX_DECAY, HY_WIDTH, dtype=F32)).reshape(1, HY_WIDTH)
    full = lambda shape: pl.BlockSpec(shape, lambda j: (0, 0))
    const = lambda shape: pl.BlockSpec(shape, lambda j: (0, 0), pipeline_mode=pl.Buffered(1))
    h = L // 2
    return pl.pallas_call(
        _filter_kernel,
        grid=(n // tc,),
        in_specs=[full((L, LANES)), full((LANES, LANES)), full((1, LANES)), full((LANES, LANES)), full((1, LANES)),
                  pl.BlockSpec((LANES, tc), lambda j: (0, j)),
                  pl.BlockSpec((LANES, tc), lambda j: (0, n // tc + j)),
                  pl.BlockSpec((1, tc), lambda j: (0, j % (HY_WIDTH // tc))),
                  const((h, h)), const((h, h))],
        out_specs=pl.BlockSpec((9, h, tc), lambda j: (0, 0, j)),
        out_shape=jax.ShapeDtypeStruct((9, h, n), F32),
        compiler_params=_cparams(("parallel",), 56),
        name="hyena_filter",
    )(z, w1p, b1p, w2p, b2p, w3p, w3p, deltas, c, s)


def _hyena_order_kernel(z_ref, g_ref, swz_ref, sbz_ref, swg_ref, sbg_ref, sk_ref, c_ref, s_ref, st_ref,
                        f_ref, o_ref, *, nchunk, conv_z):
    z = z_ref[0].astype(F32)
    if conv_z:
        z = _short_conv(z, swz_ref[...], sbz_ref[...])
    gate = _short_conv(g_ref[0].astype(F32), swg_ref[...], sbg_ref[...])
    L, tc = z.shape
    h = L // 2
    zb = z.astype(BF16)
    zcat = jnp.concatenate([zb[:h], zb[h:]], axis=1)
    fc = h // nchunk
    conv = None
    for ci in range(nchunk):
        sl = pl.ds(ci * fc, fc)
        zr = _bdot(c_ref[sl, :], zcat)
        zi = _bdot(s_ref[sl, :], zcat)
        ztr, zbr, zti, zbi = zr[:, :tc], zr[:, tc:], zi[:, :tc], zi[:, tc:]
        p0, q0, p20, p1, q1, p21, pm, qm, p2m = [f_ref[i, sl, :] for i in range(9)]
        ytr = ztr * p0 - zti * q0 + zbr * pm - zbi * qm
        yti = ztr * q0 + zti * p20 + zbr * qm + zbi * p2m
        ybr = ztr * p1 - zti * q1 + zbr * p0 - zbi * q0
        ybi = ztr * q1 + zti * p21 + zbr * q0 + zbi * p20
        yr = jnp.concatenate([ytr, ybr], axis=1).astype(BF16)
        yi = jnp.concatenate([yti, ybi], axis=1).astype(BF16)
        part = _bdot(c_ref[:, sl], yr) + _bdot(st_ref[:, sl], yi)
        conv = part if conv is None else conv + part
    conv = jnp.concatenate([conv[:, :tc], conv[:, tc:]], axis=0)
    o_ref[0] = (gate * (conv + z * sk_ref[...])).astype(o_ref.dtype)


def _hyena_order(zsrc, zblk, conv_z, p, hy_blk, part_z, part_g, short_w, short_b, skip, filt, order, dft, out_dtype):
    Bx, Lx, _ = zsrc.shape
    c, s, st = dft
    tc = 256
    nct = HY_WIDTH // tc
    taps = short_w.shape[0]
    h = Lx // 2
    const = lambda shape: pl.BlockSpec(shape, lambda j, b: (0, 0), pipeline_mode=pl.Buffered(1))
    fspec = pl.BlockSpec((9, h, tc), lambda j, b: (0, 0, order * nct + j), pipeline_mode=pl.Buffered(1))
    wspec = lambda part: pl.BlockSpec((taps, tc), lambda j, b: (0, part * nct + j))
    bspec = lambda part: pl.BlockSpec((1, tc), lambda j, b: (0, part * nct + j))
    return pl.pallas_call(
        functools.partial(_hyena_order_kernel, nchunk=max(1, h // 512), conv_z=conv_z),
        grid=(nct, Bx),
        in_specs=[pl.BlockSpec((1, Lx, tc), lambda j, b: (b, 0, zblk + j)),
                  pl.BlockSpec((1, Lx, tc), lambda j, b: (b, 0, hy_blk + part_g * nct + j)),
                  wspec(part_z), bspec(part_z), wspec(part_g), bspec(part_g),
                  pl.BlockSpec((1, tc), lambda j, b: (0, j)),
                  const((h, h)), const((h, h)), const((h, h)),
                  fspec],
        out_specs=pl.BlockSpec((1, Lx, tc), lambda j, b: (b, 0, j)),
        out_shape=jax.ShapeDtypeStruct((Bx, Lx, HY_WIDTH), out_dtype),
        compiler_params=_cparams(("parallel", "arbitrary"), 48),
        name="hyena_long_conv",
    )(zsrc, p, short_w, short_b.reshape(1, -1), short_w, short_b.reshape(1, -1), skip.reshape(1, HY_WIDTH),
      c, s, st, filt)


def _hyena(p, hy_blk, short_w, short_b, filt, skip, dft):
    z1 = _hyena_order(p, hy_blk, True, p, hy_blk, 0, 1, short_w, short_b, skip[0], filt, 0, dft, F32)
    return _hyena_order(z1, 0, False, p, hy_blk, 0, 2, short_w, short_b, skip[1], filt, 1, dft, BF16)


def _merge_kernel(a_ref, h_ref, ga_ref, gb_ref, x_ref, g1_ref, wa_ref, wb_ref, wo_ref, o_ref):
    ga = _sigmoid(ga_ref[0].astype(F32))
    gb = _sigmoid(gb_ref[0].astype(F32))
    y = ga * _bdot(a_ref[0], wa_ref[...]) + gb * _bdot(h_ref[0], wb_ref[...])
    o_ref[0] = x_ref[0] + g1_ref[0] * _bdot(y.astype(BF16), wo_ref[...])


def _merge(attn, hyena, p, gate_blk, x, g1, wa, wb, wo):
    Bx, Lx, D = x.shape
    tm = min(Lx, 512)
    W = attn.shape[2]
    const = lambda shape: pl.BlockSpec(shape, lambda b, i: (0, 0))
    return pl.pallas_call(
        _merge_kernel,
        grid=(Bx, Lx // tm),
        in_specs=[pl.BlockSpec((1, tm, W), lambda b, i: (b, i, 0)),
                  pl.BlockSpec((1, tm, W), lambda b, i: (b, i, 0)),
                  pl.BlockSpec((1, tm, D), lambda b, i: (b, i, gate_blk)),
                  pl.BlockSpec((1, tm, D), lambda b, i: (b, i, gate_blk + 1)),
                  pl.BlockSpec((1, tm, D), lambda b, i: (b, i, 0)),
                  pl.BlockSpec((1, 1, D), lambda b, i: (b, 0, 0)),
                  const((W, D)), const((W, D)), const((D, D))],
        out_specs=pl.BlockSpec((1, tm, D), lambda b, i: (b, i, 0)),
        out_shape=jax.ShapeDtypeStruct((Bx, Lx, D), F32),
        compiler_params=_cparams(("parallel", "parallel"), 48),
        name="merge_out_proj",
    )(attn, hyena, p, p, x, g1, wa, wb, wo)


MOE_TILE = 1024
MOE_CHUNK = 256
GROUP_LANE = N_EXPERTS
RANK_LANE = N_EXPERTS + 1


def _tile_rows(tm, n_tokens, seg):
    start = pl.program_id(0) * tm
    row = lax.broadcasted_iota(jnp.int32, (tm, 1), 0) + start
    return row < n_tokens, row < (start // seg + 1) * seg


def _router_kernel(x_ref, g_ref, sca_ref, sha_ref, scb_ref, shb_ref, wr_ref, br_ref, ht_ref, cw_ref, cwt_ref,
                   *, n_tokens, seg):
    valid, first = _tile_rows(x_ref.shape[1], n_tokens, seg)
    x = jnp.where(valid, x_ref[0], 0.0)
    sc = jnp.where(first, sca_ref[0], scb_ref[0])
    sh = jnp.where(first, sha_ref[0], shb_ref[0])
    h = jnp.where(valid, _rms_mod(x, g_ref[...], sc, sh), 0.0)
    ht_ref[0] = h.T.astype(BF16)
    logits = _dot3(h, wr_ref[...]) + br_ref[...]
    lane = lax.broadcasted_iota(jnp.int32, logits.shape, 1)
    lane_f = lane.astype(F32)
    big = float(LANES)
    is_g = (lane >= N_EXPERTS) & (lane < N_EXPERTS + N_GROUPS)
    gl = jnp.where(is_g, logits, NEG)
    gmax = gl.max(axis=-1, keepdims=True)
    gp = 1.0 / jnp.where(is_g, jnp.exp(gl - gmax), 0.0).sum(axis=-1, keepdims=True)
    gidx = jnp.where(is_g & (gl == gmax), lane_f - N_EXPERTS, big).min(axis=-1, keepdims=True)
    gidx = jnp.where(valid, gidx, -1.0)
    in_grp =(lane < N_EXPERTS) & ((lane // EXPERTS_PER_GROUP).astype(F32) == gidx)
    el = jnp.where(in_grp, logits, NEG)
    v1 = el.max(axis=-1, keepdims=True)
    i1 = jnp.where(in_grp & (el == v1), lane_f, big).min(axis=-1, keepdims=True)
    rest = in_grp & (lane_f != i1)
    el2 = jnp.where(rest, logits, NEG)
    v2 = el2.max(axis=-1, keepdims=True)
    i2 = jnp.where(rest & (el2 == v2), lane_f, big).min(axis=-1, keepdims=True)
    e2 = jnp.exp(v2 - v1)
    w1 = gp / (1.0 + e2)
    cw = jnp.where(lane_f == i1, w1, jnp.where(lane_f == i2, w1 * e2, 0.0))
    tm = h.shape[0]
    onehot = jnp.where((lane_f == gidx) & (lane < N_GROUPS), 1.0, 0.0)
    tri = lax.broadcasted_iota(jnp.int32, (tm, tm), 1) <= lax.broadcasted_iota(jnp.int32, (tm, tm), 0)
    cum = _bdot(jnp.where(tri, 1.0, 0.0).astype(BF16), onehot.astype(BF16))
    rank = (onehot * cum).sum(axis=-1, keepdims=True) - 1.0
    rec = jnp.where(lane == GROUP_LANE, gidx, jnp.where(lane == RANK_LANE, rank, cw))
    cw_ref[0] = rec
    cwt_ref[0] = rec.T


def _seg_spec(tm, seg, nseg, D, off):
    return pl.BlockSpec((1, 1, D), lambda i, *_: (jnp.minimum((i * tm) // seg + off, nseg - 1), 0, 0))


def _router(x, g, sc, sh, w_group, b_group, w_router, b_router):
    nseg, seg, D = x.shape
    n_tokens = nseg * seg
    tm = MOE_TILE
    nt = pl.cdiv(n_tokens, tm)
    wr = _pad2(jnp.concatenate([w_router, w_group], axis=1), D, LANES)
    br = _pad2(jnp.concatenate([b_router, b_group])[None, :], 1, LANES)
    segv = lambda off: _seg_spec(tm, seg, nseg, D, off)
    return pl.pallas_call(
        functools.partial(_router_kernel, n_tokens=n_tokens, seg=seg),
        grid=(nt,),
        in_specs=[pl.BlockSpec((1, tm, D), lambda i: (0, i, 0)),
                  pl.BlockSpec((1, D), lambda i: (0, 0)),
                  segv(0), segv(0), segv(1), segv(1),
                  pl.BlockSpec((D, LANES), lambda i: (0, 0)),
                  pl.BlockSpec((1, LANES), lambda i: (0, 0))],
        out_specs=[pl.BlockSpec((1, D, tm), lambda i: (0, 0, i)),
                   pl.BlockSpec((1, tm, LANES), lambda i: (0, i, 0)),
                   pl.BlockSpec((1, LANES, tm), lambda i: (0, 0, i))],
        out_shape=[jax.ShapeDtypeStruct((1, D, nt * tm), BF16), jax.ShapeDtypeStruct((1, nt * tm, LANES), F32),
                   jax.ShapeDtypeStruct((1, LANES, nt * tm), F32)],
        compiler_params=_cparams(("parallel",), 48),
        name="moe_router",
    )(x.reshape(1, n_tokens, D), g.reshape(1, D), sc, sh, sc, sh, wr, br)


def _moe_kernel(cnt_ref, ht_ref, cw_ref, cwt_ref, x_ref, g2a_ref, g2b_ref, w1t_ref, w3t_ref, w2t_ref, o_ref, acc_ref,
                *, n_tokens, seg):
    i, g = pl.program_id(0), pl.program_id(1)

    @pl.when(g == 0)
    def _():
        acc_ref[...] = jnp.zeros_like(acc_ref)

    n = cnt_ref[i * N_GROUPS + g]
    gf = g.astype(F32)
    tm = acc_ref.shape[1]
    c = MOE_CHUNK
    sel_col = jnp.where(cw_ref[0, :, GROUP_LANE:GROUP_LANE + 1] == gf, cw_ref[0, :, RANK_LANE:RANK_LANE + 1], -1.0)
    sel_row = jnp.where(cwt_ref[0, GROUP_LANE:GROUP_LANE + 1, :] == gf, cwt_ref[0, RANK_LANE:RANK_LANE + 1, :], -1.0)
    e0 = pl.multiple_of(g * EXPERTS_PER_GROUP, EXPERTS_PER_GROUP)
    cw_hi, cw_lo = _split(cwt_ref[0, pl.ds(e0, EXPERTS_PER_GROUP), :])
    lane_r = lax.broadcasted_iota(jnp.int32, (tm, c), 1).astype(F32)
    sub_r = lax.broadcasted_iota(jnp.int32, (c, tm), 0).astype(F32)

    def chunk(k, carry):
        r0 = (k * c).astype(F32)
        pkt = jnp.where(sel_col - r0 == lane_r, 1.0, 0.0).astype(BF16)
        pk = jnp.where(sel_row - r0 == sub_r, 1.0, 0.0).astype(BF16)
        half = ht_ref.shape[1] // 2
        xst = jnp.concatenate([_bdot(ht_ref[0, :half, :], pkt), _bdot(ht_ref[0, half:, :], pkt)],
                              axis=0).astype(BF16)
        cws = _bdot(cw_hi, pkt) + _bdot(cw_lo, pkt)
        out = None
        for e in range(EXPERTS_PER_GROUP):
            rows = slice(e * D_EXPERT, (e + 1) * D_EXPERT)
            a = _bdot(w1t_ref[0, rows, :], xst)
            hid = (a * _sigmoid(a)) * _bdot(w3t_ref[0, rows, :], xst) * cws[e:e + 1]
            part = _bdot(w2t_ref[0, :, rows], hid.astype(BF16))
            out = part if out is None else out + part
        acc_ref[...] += _bdot(out.astype(BF16), pk)
        return carry

    lax.fori_loop(0, (n + c - 1) // c, chunk, 0)

    @pl.when(g == N_GROUPS - 1)
    def _():
        _, first = _tile_rows(tm, n_tokens, seg)
        o_ref[0] = x_ref[0] + jnp.where(first, g2a_ref[0], g2b_ref[0]) * acc_ref[...].T


def _moe(ht, cw, cwt, x, g2, w1t, w3t, w2t):
    nseg, seg, D = x.shape
    n_tokens = nseg * seg
    tm = MOE_TILE
    nt = cw.shape[1] // tm
    ef = EXPERTS_PER_GROUP * D_EXPERT
    gid = cw[0, :, GROUP_LANE].reshape(nt, tm, 1)
    counts = jnp.sum(gid == jnp.arange(N_GROUPS, dtype=F32), axis=1).astype(jnp.int32).reshape(-1)
    once = pl.Buffered(1)
    segv = lambda off: _seg_spec(tm, seg, nseg, D, off)
    grid_spec = pltpu.PrefetchScalarGridSpec(
        num_scalar_prefetch=1,
        grid=(nt, N_GROUPS),
        in_specs=[pl.BlockSpec((1, D, tm), lambda i, g, cnt: (0, 0, i), pipeline_mode=once),
                  pl.BlockSpec((1, tm, LANES), lambda i, g, cnt: (0, i, 0)),
                  pl.BlockSpec((1, LANES, tm), lambda i, g, cnt: (0, 0, i)),
                  pl.BlockSpec((1, tm, D), lambda i, g, cnt: (0, i, 0), pipeline_mode=once),
                  segv(0), segv(1),
                  pl.BlockSpec((1, ef, D), lambda i, g, cnt: (g, 0, 0)),
                  pl.BlockSpec((1, ef, D), lambda i, g, cnt: (g, 0, 0)),
                  pl.BlockSpec((1, D, ef), lambda i, g, cnt: (g, 0, 0))],
        out_specs=pl.BlockSpec((1, tm, D), lambda i, g, cnt: (0, i, 0)),
        scratch_shapes=[pltpu.VMEM((D, tm), F32)])
    out = pl.pallas_call(
        functools.partial(_moe_kernel, n_tokens=n_tokens, seg=seg),
        grid_spec=grid_spec,
        out_shape=jax.ShapeDtypeStruct((1, n_tokens, D), F32),
        compiler_params=_cparams(("parallel", "arbitrary"), 56),
        name="moe_experts",
    )(counts, ht, cw, cwt, x.reshape(1, n_tokens, D), g2, g2, w1t, w3t, w2t)
    return out.reshape(nseg, seg, D)


def _layer(x, xc, mods, modc, last, lw, consts):
    B, L, D = x.shape
    Lc = xc.shape[1]
    sh1, sc1, g1, sh2, sc2, g2 = mods
    sh1c, sc1c, g1c, sh2c, sc2c, g2c = modc
    w_in = lw["w_in"]
    hy_blk = 3 * NA_WIDTH // 256
    gate_blk = (3 * NA_WIDTH + (HY_ORDER + 1) * HY_WIDTH) // D
    v_blk = 2 * NA_WIDTH // ATT_W

    p = _norm_mod_matmul(x, lw["norm_mix"], sc1, sh1, w_in)
    q_rot, q_plain, k_rot = _qk_prep(p, 0, 1, lw["q_norm"], lw["k_norm"], consts["rope"])
    if last:
        pc = _norm_mod_matmul(xc, lw["norm_mix"], sc1c, sh1c, w_in[:, NA_WIDTH:3 * NA_WIDTH])
        _, kc = _qk_prep(pc, 0, 0, lw["q_norm"], lw["k_norm"], None)
        vc_blk = NA_WIDTH // ATT_W
    else:
        pc = _norm_mod_matmul(xc, lw["norm_mix"], sc1c, sh1c, w_in)
        qc, kc = _qk_prep(pc, 0, 1, lw["q_norm"], lw["k_norm"], None)
        vc_blk = v_blk
    attn = _nbr_attention(q_rot, q_plain, k_rot, p, v_blk, kc, pc, vc_blk, _nbr_bias(lw["rpb"], L))
    flt = (lw["flt_w1"], lw["flt_b1"], lw["flt_w2"], lw["flt_b2"], lw["flt_w3"])
    filt = _hyena_filter(L, *flt, consts["dft"])
    hyena = _hyena(p, hy_blk, lw["short_w"], lw["short_b"], filt, lw["hy_skip"], consts["dft"])
    x = _merge(attn, hyena, p, gate_blk, x, g1, lw["w_br_a"], lw["w_br_b"], lw["w_out"])
    rw = (lw["w_group"], lw["b_group"], lw["w_router"], lw["b_router"])
    ew = (lw["moe_w1t"], lw["moe_w3t"], lw["moe_w2t"])
    x = _moe(*_router(x, lw["norm_ffn"], sc2, sh2, *rw), x, g2, *ew)
    if last:
        return x, xc

    attn_c = _ctx_attention(qc, kc, pc, vc_blk)
    filt_c = _hyena_filter(Lc, *flt, consts["dft_c"])
    hyena_c = _hyena(pc, hy_blk, lw["short_w"], lw["short_b"], filt_c, lw["hy_skip"], consts["dft_c"])
    xc = _merge(attn_c, hyena_c, pc, gate_blk, xc, g1c, lw["w_br_a"], lw["w_br_b"], lw["w_out"])
    xc = _moe(*_router(xc, lw["norm_ffn"], sc2c, sh2c, *rw), xc, g2c, *ew)
    return x, xc


def kernel(x, c, ctx, c_ctx, ada_w, ada_b, norm_mix, norm_ffn, w_in, q_norm, k_norm, rpb, short_w, short_b, flt_w1, flt_b1, flt_w2, flt_b2, flt_w3, hy_skip, w_br_a, w_br_b, w_out, w_group, b_group, w_router, b_router, moe_w1, moe_w3, moe_w2):
    B, L, D = x.shape
    Lc = ctx.shape[1]
    depth = ada_w.shape[0]
    consts = {"rope": _rope_tables(L), "dft": _dft_mats(L // 2), "dft_c": _dft_mats(Lc // 2)}
    rows = 8 * ((B + 1 + 7) // 8)
    cs = jnp.pad(jnp.concatenate([c, c_ctx[None, :]], axis=0), ((0, rows - B - 1), (0, 0)))
    xc = ctx
    for i in range(depth):
        mod = _ada(cs, ada_w[i], ada_b[i])
        mods = [m.reshape(B, 1, D) for m in jnp.split(mod[:B], 6, axis=-1)]
        modc = [jnp.broadcast_to(m.reshape(1, 1, D), (B, 1, D)) for m in jnp.split(mod[B], 6, axis=-1)]
        ef = EXPERTS_PER_GROUP * D_EXPERT
        lw = {
            "norm_mix": norm_mix[i], "norm_ffn": norm_ffn[i], "w_in": w_in[i].astype(BF16),
            "q_norm": q_norm[i], "k_norm": k_norm[i], "rpb": rpb[i],
            "short_w": short_w[i], "short_b": short_b[i],
            "flt_w1": flt_w1[i], "flt_b1": flt_b1[i], "flt_w2": flt_w2[i], "flt_b2": flt_b2[i], "flt_w3": flt_w3[i],
            "hy_skip": hy_skip[i],
            "w_br_a": w_br_a[i].astype(BF16), "w_br_b": w_br_b[i].astype(BF16), "w_out": w_out[i].astype(BF16),
            "w_group": w_group[i], "b_group": b_group[i], "w_router": w_router[i], "b_router": b_router[i],
            "moe_w1t": moe_w1[i].astype(BF16).transpose(0, 1, 3, 2).reshape(N_GROUPS, ef, D),
            "moe_w3t": moe_w3[i].astype(BF16).transpose(0, 1, 3, 2).reshape(N_GROUPS, ef, D),
            "moe_w2t": moe_w2[i].astype(BF16).transpose(0, 3, 1, 2).reshape(N_GROUPS, D, ef),
        }
        x, xc = _layer(x, xc, mods, modc, i == depth - 1, lw, consts)
    return x
```

```python
import functools
import math

import jax
import jax.numpy as jnp
from jax import lax
from jax.experimental import pallas as pl
from jax.experimental.pallas import tpu as pltpu

F32 = jnp.float32
BF16 = jnp.bfloat16

D_MODEL = 1024
GRID_W = 64
NA_HEADS = 8
HEAD_DIM = 64
NA_WIDTH = NA_HEADS * HEAD_DIM
WIN_H = 8
WIN_W = 16
ROPE_THETA = 100.0
HY_WIDTH = 512
HY_ORDER = 2
HY_BANDS = 16
HY_SIN_FREQ = 1.0
HY_MAX_DECAY = math.log(1e-2) / 0.3
HY_MIN_DECAY = math.log(1e-2) / 1.5
N_GROUPS = 4
EXPERTS_PER_GROUP = 8
N_EXPERTS = N_GROUPS * EXPERTS_PER_GROUP
D_EXPERT = 256
EPS = 1e-6
NEG = -1e30

LANES = 128
V7X_VMEM_BYTES = 64 * 1024 * 1024
ATT_W = 256
Q_ROWS = 4
K_ROWS = Q_ROWS + WIN_H


def _cparams(sem, vmem_mb):
    assert vmem_mb * 1024 * 1024 < V7X_VMEM_BYTES
    return pltpu.CompilerParams(dimension_semantics=sem, vmem_limit_bytes=vmem_mb * 1024 * 1024)


def _bdot(a, b):
    return jnp.dot(a, b, preferred_element_type=F32)


def _split(a):
    hi = a.astype(BF16)
    lo = (a - hi.astype(F32)).astype(BF16)
    return hi, lo


def _dot3(a, b):
    ah, al = _split(a)
    bh, bl = _split(b)
    return _bdot(ah, bh) + _bdot(ah, bl) + _bdot(al, bh)


def _sigmoid(x):
    return 1.0 / (1.0 + jnp.exp(-x))


def _rms_mod(x, g, sc, sh):
    ms = jnp.mean(x * x, axis=-1, keepdims=True)
    return (x * lax.rsqrt(ms + EPS) * g) * (1.0 + sc) + sh


def _ada_kernel(c_ref, w_ref, b_ref, o_ref):
    c = c_ref[...]
    o_ref[...] = _dot3(c * _sigmoid(c), w_ref[...]) + b_ref[...]


def _ada(cs, w, b):
    R, D = cs.shape
    N = w.shape[1]
    tn = 512
    return pl.pallas_call(
        _ada_kernel,
        grid=(N // tn,),
        in_specs=[pl.BlockSpec((R, D), lambda j: (0, 0)),
                  pl.BlockSpec((D, tn), lambda j: (0, j)),
                  pl.BlockSpec((1, tn), lambda j: (0, j))],
        out_specs=pl.BlockSpec((R, tn), lambda j: (0, j)),
        out_shape=jax.ShapeDtypeStruct((R, N), F32),
        compiler_params=_cparams(("parallel",), 32),
        name="ada_mod",
    )(cs, w, b.reshape(1, N))


def _nmm_kernel(x_ref, g_ref, sc_ref, sh_ref, w_ref, o_ref, h_ref):
    @pl.when(pl.program_id(2) == 0)
    def _():
        h_ref[...] = _rms_mod(x_ref[0], g_ref[...], sc_ref[0], sh_ref[0]).astype(BF16)

    o_ref[0] = _bdot(h_ref[...], w_ref[...]).astype(o_ref.dtype)


def _norm_mod_matmul(x, g, sc, sh, w):
    Bx, Lx, D = x.shape
    N = w.shape[1]
    tm = min(Lx, 1024)
    tn = 1280 if N % 1280 == 0 else 1024
    return pl.pallas_call(
        _nmm_kernel,
        grid=(Bx, Lx // tm, N // tn),
        in_specs=[pl.BlockSpec((1, tm, D), lambda b, i, j: (b, i, 0)),
                  pl.BlockSpec((1, D), lambda b, i, j: (0, 0)),
                  pl.BlockSpec((1, 1, D), lambda b, i, j: (b, 0, 0)),
                  pl.BlockSpec((1, 1, D), lambda b, i, j: (b, 0, 0)),
                  pl.BlockSpec((D, tn), lambda b, i, j: (0, j))],
        out_specs=pl.BlockSpec((1, tm, tn), lambda b, i, j: (b, i, j)),
        out_shape=jax.ShapeDtypeStruct((Bx, Lx, N), BF16),
        scratch_shapes=[pltpu.VMEM((tm, D), BF16)],
        compiler_params=_cparams(("parallel", "parallel", "arbitrary"), 40),
        name="norm_mod_proj",
    )(x, g.reshape(1, D), sc, sh, w)


def _head_norm(x, gn, bd):
    hi, lo = _split(x * x)
    ms = _bdot(hi, bd) + _bdot(lo, bd)
    return x * lax.rsqrt(ms + EPS) * gn


def _rope(x, cos, sin_signed):
    lane = lax.broadcasted_iota(jnp.int32, x.shape, 1)
    quarter = HEAD_DIM // 4
    partner = jnp.where((lane % (2 * quarter)) < quarter,
                        pltpu.roll(x, x.shape[1] - quarter, 1), pltpu.roll(x, quarter, 1))
    return x * cos + partner * sin_signed


def _qk_rope_kernel(q_ref, k_ref, qn_ref, kn_ref, cos_ref, sin_ref, bd_ref, qr_ref, qp_ref, kr_ref):
    bd = bd_ref[...]
    cos = cos_ref[...]
    sin = sin_ref[...]
    q = _head_norm(q_ref[0].astype(F32), qn_ref[...], bd) * (HEAD_DIM ** -0.5)
    k = _head_norm(k_ref[0].astype(F32), kn_ref[...], bd)
    qp_ref[0] = q.astype(BF16)
    qr_ref[0] = _rope(q, cos, sin).astype(BF16)
    kr_ref[0] = _rope(k, cos, sin).astype(BF16)


def _qk_plain_kernel(q_ref, k_ref, qn_ref, kn_ref, bd_ref, qp_ref, kp_ref):
    bd = bd_ref[...]
    qp_ref[0] = (_head_norm(q_ref[0].astype(F32), qn_ref[...], bd) * (HEAD_DIM ** -0.5)).astype(BF16)
    kp_ref[0] = _head_norm(k_ref[0].astype(F32), kn_ref[...], bd).astype(BF16)


def _head_block_diag():
    r = jnp.arange(NA_WIDTH) // HEAD_DIM
    return jnp.where(r[:, None] == r[None, :], 1.0 / HEAD_DIM, 0.0).astype(BF16)


def _qk_prep(p, qcol, kcol, qn, kn, rope_tabs):
    Bx, Lx, _ = p.shape
    W = NA_WIDTH
    tm = min(Lx, 512)
    tok = lambda c: pl.BlockSpec((1, tm, W), lambda b, i: (b, i, c))
    vec = pl.BlockSpec((1, W), lambda b, i: (0, 0))
    mat = pl.BlockSpec((W, W), lambda b, i: (0, 0))
    out = pl.BlockSpec((1, tm, W), lambda b, i: (b, i, 0))
    osd = jax.ShapeDtypeStruct((Bx, Lx, W), BF16)
    qn_t = jnp.tile(qn, NA_HEADS).reshape(1, W)
    kn_t = jnp.tile(kn, NA_HEADS).reshape(1, W)
    bd = _head_block_diag()
    if rope_tabs is None:
        return pl.pallas_call(
            _qk_plain_kernel, grid=(Bx, Lx // tm),
            in_specs=[tok(qcol), tok(kcol), vec, vec, mat],
            out_specs=[out, out], out_shape=[osd, osd],
            compiler_params=_cparams(("parallel", "parallel"), 32), name="qk_norm",
        )(p, p, qn_t, kn_t, bd)
    cos, sin = rope_tabs
    tab = pl.BlockSpec((tm, W), lambda b, i: (i, 0))
    return pl.pallas_call(
        _qk_rope_kernel, grid=(Bx, Lx // tm),
        in_specs=[tok(qcol), tok(kcol), vec, vec, tab, tab, mat],
        out_specs=[out, out, out], out_shape=[osd, osd, osd],
        compiler_params=_cparams(("parallel", "parallel"), 32), name="qk_norm_rope",
    )(p, p, qn_t, kn_t, cos, sin, bd)


def _rope_tables(L):
    quarter = HEAD_DIM // 4
    freqs = ROPE_THETA ** (-jnp.arange(quarter, dtype=F32) / quarter)
    pos = jnp.arange(L)
    rows, cols = (pos // GRID_W).astype(F32), (pos % GRID_W).astype(F32)
    d = jnp.arange(NA_WIDTH) % HEAD_DIM
    p = jnp.where((d < HEAD_DIM // 2)[None, :], rows[:, None], cols[:, None])
    ang = p * freqs[d % quarter][None, :]
    sign = jnp.where((d % (2 * quarter)) < quarter, -1.0, 1.0)[None, :]
    return jnp.cos(ang), jnp.sin(ang) * sign


def _softmax_pv(s_list, v_list):
    m = s_list[0].max(axis=-1, keepdims=True)
    for s in s_list[1:]:
        m = jnp.maximum(m, s.max(axis=-1, keepdims=True))
    l = 0.0
    o = 0.0
    for s, v in zip(s_list, v_list):
        p = jnp.exp(s - m)
        l = l + p.sum(axis=-1, keepdims=True)
        o = o + _bdot(p.astype(BF16), v)
    return o / l


def _qkt(q, k):
    return lax.dot_general(q, k, (((1,), (1,)), ((), ())), preferred_element_type=F32)


def _head_lanes(x, lane, hh):
    return jnp.where((lane // HEAD_DIM) == hh, x, 0.0).astype(BF16)


def _nbr_attn_kernel(q_ref, qp_ref, k_ref, v_ref, kc_ref, vc_ref, bias_ref, o_ref):
    i = pl.program_id(1)
    max_row0 = k_ref.shape[1] // GRID_W - K_ROWS
    k0 = pl.multiple_of(jnp.clip(Q_ROWS * i - WIN_H // 2, 0, max_row0) * GRID_W, 4 * GRID_W)
    nk = K_ROWS * GRID_W
    k = k_ref[0, pl.ds(k0, nk), :]
    v = v_ref[0, pl.ds(k0, nk), :].astype(BF16)
    kc = kc_ref[0]
    vc = vc_ref[0].astype(BF16)
    q = q_ref[0].astype(F32)
    qp = qp_ref[0].astype(F32)
    lane = lax.broadcasted_iota(jnp.int32, q.shape, 1)
    out = None
    scores = []
    for hh in range(ATT_W // HEAD_DIM):
        scores.append((_qkt(_head_lanes(q, lane, hh), k) + bias_ref[hh, 0], _qkt(_head_lanes(qp, lane, hh), kc)))
    for hh in range(ATT_W // HEAD_DIM):
        o = _softmax_pv(list(scores[hh]), [v, vc])
        out = o if out is None else jnp.where((lane // HEAD_DIM) == hh, o, out)
    o_ref[0] = out.astype(o_ref.dtype)


def _nbr_attention(q_rot, q_plain, k_rot, p, v_blk, kc, pc, vc_blk, bias):
    B, L, _ = q_rot.shape
    Lc = kc.shape[1]
    nq = Q_ROWS * GRID_W
    ni = L // nq
    return pl.pallas_call(
        _nbr_attn_kernel,
        grid=(NA_WIDTH // ATT_W, ni, B),
        in_specs=[pl.BlockSpec((1, nq, ATT_W), lambda h, i, b: (b, i, h)),
                  pl.BlockSpec((1, nq, ATT_W), lambda h, i, b: (b, i, h)),
                  pl.BlockSpec((1, L, ATT_W), lambda h, i, b: (b, 0, h)),
                  pl.BlockSpec((1, L, ATT_W), lambda h, i, b: (b, 0, v_blk + h)),
                  pl.BlockSpec((1, Lc, ATT_W), lambda h, i, b: (b, 0, h)),
                  pl.BlockSpec((1, Lc, ATT_W), lambda h, i, b: (b, 0, vc_blk + h)),
                  pl.BlockSpec((ATT_W // HEAD_DIM, 1, nq, K_ROWS * GRID_W), lambda h, i, b: (h, i, 0, 0))],
        out_specs=pl.BlockSpec((1, nq, ATT_W), lambda h, i, b: (b, i, h)),
        out_shape=jax.ShapeDtypeStruct((B, L, NA_WIDTH), BF16),
        compiler_params=_cparams(("parallel", "parallel", "arbitrary"), 48),
        name="nbr_attention",
    )(q_rot, q_plain, k_rot, p, kc, pc, bias)


def _ctx_attn_kernel(q_ref, k_ref, v_ref, o_ref):
    q = q_ref[0].astype(F32)
    k = k_ref[0]
    v = v_ref[0].astype(BF16)
    lane = lax.broadcasted_iota(jnp.int32, q.shape, 1)
    out = None
    for hh in range(ATT_W // HEAD_DIM):
        o = _softmax_pv([_qkt(_head_lanes(q, lane, hh), k)], [v])
        out = o if out is None else jnp.where((lane // HEAD_DIM) == hh, o, out)
    o_ref[0] = out.astype(o_ref.dtype)


def _ctx_attention(qc, kc, pc, vc_blk):
    B, Lc, _ = qc.shape
    blk = lambda off: pl.BlockSpec((1, Lc, ATT_W), lambda h, b: (b, 0, off + h))
    return pl.pallas_call(
        _ctx_attn_kernel,
        grid=(NA_WIDTH // ATT_W, B),
        in_specs=[blk(0), blk(0), blk(vc_blk)],
        out_specs=blk(0),
        out_shape=jax.ShapeDtypeStruct((B, Lc, NA_WIDTH), BF16),
        compiler_params=_cparams(("parallel", "parallel"), 32),
        name="ctx_attention",
    )(qc, kc, pc)


def _nbr_bias(rpb, L):
    R = L // GRID_W
    kh = min(WIN_H, R)
    qc = jnp.arange(GRID_W)[:, None]
    kcol = jnp.arange(GRID_W)[None, :]
    wstart = jnp.clip(qc - WIN_W // 2, 0, GRID_W - WIN_W)
    col_ok = (kcol >= wstart) & (kcol < wstart + WIN_W)
    ext = GRID_W - WIN_W
    rpb_ext = jnp.pad(rpb.astype(F32), ((0, 0), (0, 0), (ext, ext)), mode="edge")
    band = jnp.stack([rpb_ext[:, :, GRID_W - 1 - c:2 * GRID_W - 1 - c] for c in range(GRID_W)], axis=2)
    t = jnp.where(col_ok[None, None], band, NEG)
    t_cat = t.transpose(0, 2, 1, 3).reshape(NA_HEADS, GRID_W, (2 * WIN_H - 1) * GRID_W)
    rows = []
    for r in range(R):
        k0 = min(max(Q_ROWS * (r // Q_ROWS) - WIN_H // 2, 0), R - K_ROWS)
        rstart = min(max(r - kh // 2, 0), R - kh)
        a_lo = rstart - r + (WIN_H - 1)
        win = t_cat[:, :, a_lo * GRID_W:(a_lo + kh) * GRID_W]
        left = (rstart - k0) * GRID_W
        right = (K_ROWS - kh) * GRID_W - left
        rows.append(jnp.pad(win, ((0, 0), (0, 0), (left, right)), constant_values=NEG))
    return jnp.stack(rows, axis=1).reshape(NA_HEADS, R // Q_ROWS, Q_ROWS * GRID_W, K_ROWS * GRID_W)


def _short_conv(u, w, b):
    n = u.shape[0]
    row = lax.broadcasted_iota(jnp.int32, u.shape, 0)
    prev = jnp.where(row == 0, 0.0, pltpu.roll(u, 1, 0))
    nxt = jnp.where(row == n - 1, 0.0, pltpu.roll(u, n - 1, 0))
    return prev * w[0:1] + u * w[1:2] + nxt * w[2:3] + b


def _dft_mats(L):
    n = 2 * L
    f = jnp.arange(L, dtype=jnp.int32)[:, None]
    s = jnp.arange(L, dtype=jnp.int32)[None, :]
    ang = ((f * s) % n).astype(F32) * (2.0 * math.pi / n)
    alt = jnp.where(s % 2 == 0, 1.0, -1.0).astype(F32)
    c = jnp.cos(ang)
    sm = jnp.where(f == 0, alt, -jnp.sin(ang))
    return c.astype(BF16), sm.astype(BF16), sm.T.astype(BF16)


def _filter_features(L):
    pos = jnp.arange(L, dtype=F32)
    t = pos / max(L - 1, 1)
    w = 2.0 * math.pi * pos / L
    f = jnp.linspace(1e-4, HY_BANDS - 1, HY_BANDS, dtype=F32)
    z = jnp.concatenate([t[:, None], jnp.cos(f[None, :] * w[:, None]), -jnp.sin(f[None, :] * w[:, None])], axis=-1)
    return jnp.pad(z, ((0, 0), (0, LANES - z.shape[1])))


def _filter_kernel(z_ref, w1_ref, b1_ref, w2_ref, b2_ref, w3f_ref, w3b_ref, dl_ref, c_ref, s_ref, o_ref):
    L = z_ref.shape[0]
    hid = jnp.sin(HY_SIN_FREQ * (_dot3(z_ref[...], w1_ref[...]) + b1_ref[...]))
    hid = jnp.sin(HY_SIN_FREQ * (_dot3(hid, w2_ref[...]) + b2_ref[...]))
    hf = _dot3(hid, w3f_ref[...])
    hb = _dot3(hid, w3b_ref[...])
    row = lax.broadcasted_iota(jnp.int32, hf.shape, 0)
    t = row.astype(F32) / float(max(L - 1, 1))
    dec = jnp.exp(-t * dl_ref[...])
    hf = hf * dec
    hb = jnp.where(row == 0, 0.0, hb * dec)
    inv = 1.0 / (jnp.sum(jnp.abs(hf), axis=0, keepdims=True) + jnp.sum(jnp.abs(hb), axis=0, keepdims=True) + EPS)
    h = L // 2
    tc = hf.shape[1]
    hf = hf * inv
    hb = hb * inv
    rowh = lax.broadcasted_iota(jnp.int32, (h, tc), 0)
    f0 = hf[:h]
    x = jnp.concatenate([f0, jnp.where(rowh == 0, 0.0, f0), hf[h:], hb[:h], hb[h:]], axis=1).astype(BF16)
    re = _bdot(c_ref[...], x)
    im = _bdot(s_ref[...], x)
    rf0, rf0z, rf1, rb0, rb1 = [re[:, i * tc:(i + 1) * tc] for i in range(5)]
    if0, if0z, if1, ib0, ib1 = [im[:, i * tc:(i + 1) * tc] for i in range(5)]
    sgn = jnp.where(rowh % 2 == 0, 1.0, -1.0)
    conj = lambda i: jnp.where(rowh == 0, i, -i)
    blocks = [(rf0 + rb0, if0 + conj(ib0)),
              (rf1 + sgn * rf0z, if1 + sgn * if0z),
              (rb1 + sgn * rb0, conj(ib1) + sgn * conj(ib0))]
    alpha = jnp.where(rowh == 0, 1.0 / L, 2.0 / L)
    for i, (kr, ki) in enumerate(blocks):
        o_ref[3 * i] = kr * alpha
        o_ref[3 * i + 1] = jnp.where(rowh == 0, 0.0, ki) * alpha
        o_ref[3 * i + 2] = jnp.where(rowh == 0, ki, kr) * alpha


def _pad2(a, r, c):
    return jnp.pad(a, ((0, r - a.shape[0]), (0, c - a.shape[1])))


def _hyena_filter(L, w1, b1, w2, b2, w3, dft):
    c, s, _ = dft
    n = HY_ORDER * HY_WIDTH
    tc = 256
    z = _filter_features(L)
    w1p = _pad2(w1, LANES, LANES)
    w2p = _pad2(w2, LANES, LANES)
    w3p = _pad2(w3, LANES, 2 * n)
    b1p = _pad2(b1[None, :], 1, LANES)
    b2p = _pad2(b2[None, :], 1, LANES)
    deltas = jnp.abs(jnp.linspace(HY_MIN_DECAY, HY_MAX_DECAY, HY_WIDTH, dtype=F32)).reshape(1, HY_WIDTH)
    full = lambda shape: pl.BlockSpec(shape, lambda j: (0, 0))
    const = lambda shape: pl.BlockSpec(shape, lambda j: (0, 0), pipeline_mode=pl.Buffered(1))
    h = L // 2
    return pl.pallas_call(
        _filter_kernel,
        grid=(n // tc,),
        in_specs=[full((L, LANES)), full((LANES, LANES)), full((1, LANES)), full((LANES, LANES)), full((1, LANES)),
                  pl.BlockSpec((LANES, tc), lambda j: (0, j)),
                  pl.BlockSpec((LANES, tc), lambda j: (0, n // tc + j)),
                  pl.BlockSpec((1, tc), lambda j: (0, j % (HY_WIDTH // tc))),
                  const((h, h)), const((h, h))],
        out_specs=pl.BlockSpec((9, h, tc), lambda j: (0, 0, j)),
        out_shape=jax.ShapeDtypeStruct((9, h, n), F32),
        compiler_params=_cparams(("parallel",), 56),
        name="hyena_filter",
    )(z, w1p, b1p, w2p, b2p, w3p, w3p, deltas, c, s)


def _hyena_order_kernel(z_ref, g_ref, swz_ref, sbz_ref, swg_ref, sbg_ref, sk_ref, c_ref, s_ref, st_ref,
                        f_ref, o_ref, *, nchunk, conv_z):
    z = z_ref[0].astype(F32)
    if conv_z:
        z = _short_conv(z, swz_ref[...], sbz_ref[...])
    gate = _short_conv(g_ref[0].astype(F32), swg_ref[...], sbg_ref[...])
    L, tc = z.shape
    h = L // 2
    zb = z.astype(BF16)
    zcat = jnp.concatenate([zb[:h], zb[h:]], axis=1)
    fc = h // nchunk
    conv = None
    for ci in range(nchunk):
        sl = pl.ds(ci * fc, fc)
        zr = _bdot(c_ref[sl, :], zcat)
        zi = _bdot(s_ref[sl, :], zcat)
        ztr, zbr, zti, zbi = zr[:, :tc], zr[:, tc:], zi[:, :tc], zi[:, tc:]
        p0, q0, p20, p1, q1, p21, pm, qm, p2m = [f_ref[i, sl, :] for i in range(9)]
        ytr = ztr * p0 - zti * q0 + zbr * pm - zbi * qm
        yti = ztr * q0 + zti * p20 + zbr * qm + zbi * p2m
        ybr = ztr * p1 - zti * q1 + zbr * p0 - zbi * q0
        ybi = ztr * q1 + zti * p21 + zbr * q0 + zbi * p20
        yr = jnp.concatenate([ytr, ybr], axis=1).astype(BF16)
        yi = jnp.concatenate([yti, ybi], axis=1).astype(BF16)
        part = _bdot(c_ref[:, sl], yr) + _bdot(st_ref[:, sl], yi)
        conv = part if conv is None else conv + part
    conv = jnp.concatenate([conv[:, :tc], conv[:, tc:]], axis=0)
    o_ref[0] = (gate * (conv + z * sk_ref[...])).astype(o_ref.dtype)


def _hyena_order(zsrc, zblk, conv_z, p, hy_blk, part_z, part_g, short_w, short_b, skip, filt, order, dft, out_dtype):
    Bx, Lx, _ = zsrc.shape
    c, s, st = dft
    tc = 256
    nct = HY_WIDTH // tc
    taps = short_w.shape[0]
    h = Lx // 2
    const = lambda shape: pl.BlockSpec(shape, lambda j, b: (0, 0), pipeline_mode=pl.Buffered(1))
    fspec = pl.BlockSpec((9, h, tc), lambda j, b: (0, 0, order * nct + j), pipeline_mode=pl.Buffered(1))
    wspec = lambda part: pl.BlockSpec((taps, tc), lambda j, b: (0, part * nct + j))
    bspec = lambda part: pl.BlockSpec((1, tc), lambda j, b: (0, part * nct + j))
    return pl.pallas_call(
        functools.partial(_hyena_order_kernel, nchunk=max(1, h // 512), conv_z=conv_z),
        grid=(nct, Bx),
        in_specs=[pl.BlockSpec((1, Lx, tc), lambda j, b: (b, 0, zblk + j)),
                  pl.BlockSpec((1, Lx, tc), lambda j, b: (b, 0, hy_blk + part_g * nct + j)),
                  wspec(part_z), bspec(part_z), wspec(part_g), bspec(part_g),
                  pl.BlockSpec((1, tc), lambda j, b: (0, j)),
                  const((h, h)), const((h, h)), const((h, h)),
                  fspec],
        out_specs=pl.BlockSpec((1, Lx, tc), lambda j, b: (b, 0, j)),
        out_shape=jax.ShapeDtypeStruct((Bx, Lx, HY_WIDTH), out_dtype),
        compiler_params=_cparams(("parallel", "arbitrary"), 48),
        name="hyena_long_conv",
    )(zsrc, p, short_w, short_b.reshape(1, -1), short_w, short_b.reshape(1, -1), skip.reshape(1, HY_WIDTH),
      c, s, st, filt)


def _hyena(p, hy_blk, short_w, short_b, filt, skip, dft):
    z1 = _hyena_order(p, hy_blk, True, p, hy_blk, 0, 1, short_w, short_b, skip[0], filt, 0, dft, F32)
    return _hyena_order(z1, 0, False, p, hy_blk, 0, 2, short_w, short_b, skip[1], filt, 1, dft, BF16)


def _merge_kernel(a_ref, h_ref, ga_ref, gb_ref, x_ref, g1_ref, wa_ref, wb_ref, wo_ref, o_ref):
    ga = _sigmoid(ga_ref[0].astype(F32))
    gb = _sigmoid(gb_ref[0].astype(F32))
    y = ga * _bdot(a_ref[0], wa_ref[...]) + gb * _bdot(h_ref[0], wb_ref[...])
    o_ref[0] = x_ref[0] + g1_ref[0] * _bdot(y.astype(BF16), wo_ref[...])


def _merge(attn, hyena, p, gate_blk, x, g1, wa, wb, wo):
    Bx, Lx, D = x.shape
    tm = min(Lx, 512)
    W = attn.shape[2]
    const = lambda shape: pl.BlockSpec(shape, lambda b, i: (0, 0))
    return pl.pallas_call(
        _merge_kernel,
        grid=(Bx, Lx // tm),
        in_specs=[pl.BlockSpec((1, tm, W), lambda b, i: (b, i, 0)),
                  pl.BlockSpec((1, tm, W), lambda b, i: (b, i, 0)),
                  pl.BlockSpec((1, tm, D), lambda b, i: (b, i, gate_blk)),
                  pl.BlockSpec((1, tm, D), lambda b, i: (b, i, gate_blk + 1)),
                  pl.BlockSpec((1, tm, D), lambda b, i: (b, i, 0)),
                  pl.BlockSpec((1, 1, D), lambda b, i: (b, 0, 0)),
                  const((W, D)), const((W, D)), const((D, D))],
        out_specs=pl.BlockSpec((1, tm, D), lambda b, i: (b, i, 0)),
        out_shape=jax.ShapeDtypeStruct((Bx, Lx, D), F32),
        compiler_params=_cparams(("parallel", "parallel"), 48),
        name="merge_out_proj",
    )(attn, hyena, p, p, x, g1, wa, wb, wo)


MOE_TILE = 1024
MOE_CHUNK = 256
GROUP_LANE = N_EXPERTS
RANK_LANE = N_EXPERTS + 1


def _tile_rows(tm, n_tokens, seg):
    start = pl.program_id(0) * tm
    row = lax.broadcasted_iota(jnp.int32, (tm, 1), 0) + start
    return row < n_tokens, row < (start // seg + 1) * seg


def _router_kernel(x_ref, g_ref, sca_ref, sha_ref, scb_ref, shb_ref, wr_ref, br_ref, ht_ref, cw_ref, cwt_ref,
                   *, n_tokens, seg):
    valid, first = _tile_rows(x_ref.shape[1], n_tokens, seg)
    x = jnp.where(valid, x_ref[0], 0.0)
    sc = jnp.where(first, sca_ref[0], scb_ref[0])
    sh = jnp.where(first, sha_ref[0], shb_ref[0])
    h = jnp.where(valid, _rms_mod(x, g_ref[...], sc, sh), 0.0)
    ht_ref[0] = h.T.astype(BF16)
    logits = _dot3(h, wr_ref[...]) + br_ref[...]
    lane = lax.broadcasted_iota(jnp.int32, logits.shape, 1)
    lane_f = lane.astype(F32)
    big = float(LANES)
    is_g = (lane >= N_EXPERTS) & (lane < N_EXPERTS + N_GROUPS)
    gl = jnp.where(is_g, logits, NEG)
    gmax = gl.max(axis=-1, keepdims=True)
    gp = 1.0 / jnp.where(is_g, jnp.exp(gl - gmax), 0.0).sum(axis=-1, keepdims=True)
    gidx = jnp.where(is_g & (gl == gmax), lane_f - N_EXPERTS, big).min(axis=-1, keepdims=True)
    gidx = jnp.where(valid, gidx, -1.0)
    in_grp =(lane < N_EXPERTS) & ((lane // EXPERTS_PER_GROUP).astype(F32) == gidx)
    el = jnp.where(in_grp, logits, NEG)
    v1 = el.max(axis=-1, keepdims=True)
    i1 = jnp.where(in_grp & (el == v1), lane_f, big).min(axis=-1, keepdims=True)
    rest = in_grp & (lane_f != i1)
    el2 = jnp.where(rest, logits, NEG)
    v2 = el2.max(axis=-1, keepdims=True)
    i2 = jnp.where(rest & (el2 == v2), lane_f, big).min(axis=-1, keepdims=True)
    e2 = jnp.exp(v2 - v1)
    w1 = gp / (1.0 + e2)
    cw = jnp.where(lane_f == i1, w1, jnp.where(lane_f == i2, w1 * e2, 0.0))
    tm = h.shape[0]
    onehot = jnp.where((lane_f == gidx) & (lane < N_GROUPS), 1.0, 0.0)
    tri = lax.broadcasted_iota(jnp.int32, (tm, tm), 1) <= lax.broadcasted_iota(jnp.int32, (tm, tm), 0)
    cum = _bdot(jnp.where(tri, 1.0, 0.0).astype(BF16), onehot.astype(BF16))
    rank = (onehot * cum).sum(axis=-1, keepdims=True) - 1.0
    rec = jnp.where(lane == GROUP_LANE, gidx, jnp.where(lane == RANK_LANE, rank, cw))
    cw_ref[0] = rec
    cwt_ref[0] = rec.T


def _seg_spec(tm, seg, nseg, D, off):
    return pl.BlockSpec((1, 1, D), lambda i, *_: (jnp.minimum((i * tm) // seg + off, nseg - 1), 0, 0))


def _router(x, g, sc, sh, w_group, b_group, w_router, b_router):
    nseg, seg, D = x.shape
    n_tokens = nseg * seg
    tm = MOE_TILE
    nt = pl.cdiv(n_tokens, tm)
    wr = _pad2(jnp.concatenate([w_router, w_group], axis=1), D, LANES)
    br = _pad2(jnp.concatenate([b_router, b_group])[None, :], 1, LANES)
    segv = lambda off: _seg_spec(tm, seg, nseg, D, off)
    return pl.pallas_call(
        functools.partial(_router_kernel, n_tokens=n_tokens, seg=seg),
        grid=(nt,),
        in_specs=[pl.BlockSpec((1, tm, D), lambda i: (0, i, 0)),
                  pl.BlockSpec((1, D), lambda i: (0, 0)),
                  segv(0), segv(0), segv(1), segv(1),
                  pl.BlockSpec((D, LANES), lambda i: (0, 0)),
                  pl.BlockSpec((1, LANES), lambda i: (0, 0))],
        out_specs=[pl.BlockSpec((1, D, tm), lambda i: (0, 0, i)),
                   pl.BlockSpec((1, tm, LANES), lambda i: (0, i, 0)),
                   pl.BlockSpec((1, LANES, tm), lambda i: (0, 0, i))],
        out_shape=[jax.ShapeDtypeStruct((1, D, nt * tm), BF16), jax.ShapeDtypeStruct((1, nt * tm, LANES), F32),
                   jax.ShapeDtypeStruct((1, LANES, nt * tm), F32)],
        compiler_params=_cparams(("parallel",), 48),
        name="moe_router",
    )(x.reshape(1, n_tokens, D), g.reshape(1, D), sc, sh, sc, sh, wr, br)


def _moe_kernel(cnt_ref, ht_ref, cw_ref, cwt_ref, x_ref, g2a_ref, g2b_ref, w1t_ref, w3t_ref, w2t_ref, o_ref, acc_ref,
                *, n_tokens, seg):
    i, g = pl.program_id(0), pl.program_id(1)

    @pl.when(g == 0)
    def _():
        acc_ref[...] = jnp.zeros_like(acc_ref)

    n = cnt_ref[i * N_GROUPS + g]
    gf = g.astype(F32)
    tm = acc_ref.shape[1]
    c = MOE_CHUNK
    sel_col = jnp.where(cw_ref[0, :, GROUP_LANE:GROUP_LANE + 1] == gf, cw_ref[0, :, RANK_LANE:RANK_LANE + 1], -1.0)
    sel_row = jnp.where(cwt_ref[0, GROUP_LANE:GROUP_LANE + 1, :] == gf, cwt_ref[0, RANK_LANE:RANK_LANE + 1, :], -1.0)
    e0 = pl.multiple_of(g * EXPERTS_PER_GROUP, EXPERTS_PER_GROUP)
    cw_hi, cw_lo = _split(cwt_ref[0, pl.ds(e0, EXPERTS_PER_GROUP), :])
    lane_r = lax.broadcasted_iota(jnp.int32, (tm, c), 1).astype(F32)
    sub_r = lax.broadcasted_iota(jnp.int32, (c, tm), 0).astype(F32)

    def chunk(k, carry):
        r0 = (k * c).astype(F32)
        pkt = jnp.where(sel_col - r0 == lane_r, 1.0, 0.0).astype(BF16)
        pk = jnp.where(sel_row - r0 == sub_r, 1.0, 0.0).astype(BF16)
        half = ht_ref.shape[1] // 2
        xst = jnp.concatenate([_bdot(ht_ref[0, :half, :], pkt), _bdot(ht_ref[0, half:, :], pkt)],
                              axis=0).astype(BF16)
        cws = _bdot(cw_hi, pkt) + _bdot(cw_lo, pkt)
        out = None
        for e in range(EXPERTS_PER_GROUP):
            rows = slice(e * D_EXPERT, (e + 1) * D_EXPERT)
            a = _bdot(w1t_ref[0, rows, :], xst)
            hid = (a * _sigmoid(a)) * _bdot(w3t_ref[0, rows, :], xst) * cws[e:e + 1]
            part = _bdot(w2t_ref[0, :, rows], hid.astype(BF16))
            out = part if out is None else out + part
        acc_ref[...] += _bdot(out.astype(BF16), pk)
        return carry

    lax.fori_loop(0, (n + c - 1) // c, chunk, 0)

    @pl.when(g == N_GROUPS - 1)
    def _():
        _, first = _tile_rows(tm, n_tokens, seg)
        o_ref[0] = x_ref[0] + jnp.where(first, g2a_ref[0], g2b_ref[0]) * acc_ref[...].T


def _moe(ht, cw, cwt, x, g2, w1t, w3t, w2t):
    nseg, seg, D = x.shape
    n_tokens = nseg * seg
    tm = MOE_TILE
    nt = cw.shape[1] // tm
    ef = EXPERTS_PER_GROUP * D_EXPERT
    gid = cw[0, :, GROUP_LANE].reshape(nt, tm, 1)
    counts = jnp.sum(gid == jnp.arange(N_GROUPS, dtype=F32), axis=1).astype(jnp.int32).reshape(-1)
    once = pl.Buffered(1)
    segv = lambda off: _seg_spec(tm, seg, nseg, D, off)
    grid_spec = pltpu.PrefetchScalarGridSpec(
        num_scalar_prefetch=1,
        grid=(nt, N_GROUPS),
        in_specs=[pl.BlockSpec((1, D, tm), lambda i, g, cnt: (0, 0, i), pipeline_mode=once),
                  pl.BlockSpec((1, tm, LANES), lambda i, g, cnt: (0, i, 0)),
                  pl.BlockSpec((1, LANES, tm), lambda i, g, cnt: (0, 0, i)),
                  pl.BlockSpec((1, tm, D), lambda i, g, cnt: (0, i, 0), pipeline_mode=once),
                  segv(0), segv(1),
                  pl.BlockSpec((1, ef, D), lambda i, g, cnt: (g, 0, 0)),
                  pl.BlockSpec((1, ef, D), lambda i, g, cnt: (g, 0, 0)),
                  pl.BlockSpec((1, D, ef), lambda i, g, cnt: (g, 0, 0))],
        out_specs=pl.BlockSpec((1, tm, D), lambda i, g, cnt: (0, i, 0)),
        scratch_shapes=[pltpu.VMEM((D, tm), F32)])
    out = pl.pallas_call(
        functools.partial(_moe_kernel, n_tokens=n_tokens, seg=seg),
        grid_spec=grid_spec,
        out_shape=jax.ShapeDtypeStruct((1, n_tokens, D), F32),
        compiler_params=_cparams(("parallel", "arbitrary"), 56),
        name="moe_experts",
    )(counts, ht, cw, cwt, x.reshape(1, n_tokens, D), g2, g2, w1t, w3t, w2t)
    return out.reshape(nseg, seg, D)


def _layer(x, xc, mods, modc, last, lw, consts):
    B, L, D = x.shape
    Lc = xc.shape[1]
    sh1, sc1, g1, sh2, sc2, g2 = mods
    sh1c, sc1c, g1c, sh2c, sc2c, g2c = modc
    w_in = lw["w_in"]
    hy_blk = 3 * NA_WIDTH // 256
    gate_blk = (3 * NA_WIDTH + (HY_ORDER + 1) * HY_WIDTH) // D
    v_blk = 2 * NA_WIDTH // ATT_W

    p = _norm_mod_matmul(x, lw["norm_mix"], sc1, sh1, w_in)
    q_rot, q_plain, k_rot = _qk_prep(p, 0, 1, lw["q_norm"], lw["k_norm"], consts["rope"])
    if last:
        pc = _norm_mod_matmul(xc, lw["norm_mix"], sc1c, sh1c, w_in[:, NA_WIDTH:3 * NA_WIDTH])
        _, kc = _qk_prep(pc, 0, 0, lw["q_norm"], lw["k_norm"], None)
        vc_blk = NA_WIDTH // ATT_W
    else:
        pc = _norm_mod_matmul(xc, lw["norm_mix"], sc1c, sh1c, w_in)
        qc, kc = _qk_prep(pc, 0, 1, lw["q_norm"], lw["k_norm"], None)
        vc_blk = v_blk
    attn = _nbr_attention(q_rot, q_plain, k_rot, p, v_blk, kc, pc, vc_blk, _nbr_bias(lw["rpb"], L))
    flt = (lw["flt_w1"], lw["flt_b1"], lw["flt_w2"], lw["flt_b2"], lw["flt_w3"])
    filt = _hyena_filter(L, *flt, consts["dft"])
    hyena = _hyena(p, hy_blk, lw["short_w"], lw["short_b"], filt, lw["hy_skip"], consts["dft"])
    x = _merge(attn, hyena, p, gate_blk, x, g1, lw["w_br_a"], lw["w_br_b"], lw["w_out"])
    rw = (lw["w_group"], lw["b_group"], lw["w_router"], lw["b_router"])
    ew = (lw["moe_w1t"], lw["moe_w3t"], lw["moe_w2t"])
    x = _moe(*_router(x, lw["norm_ffn"], sc2, sh2, *rw), x, g2, *ew)
    if last:
        return x, xc

    attn_c = _ctx_attention(qc, kc, pc, vc_blk)
    filt_c = _hyena_filter(Lc, *flt, consts["dft_c"])
    hyena_c = _hyena(pc, hy_blk, lw["short_w"], lw["short_b"], filt_c, lw["hy_skip"], consts["dft_c"])
    xc = _merge(attn_c, hyena_c, pc, gate_blk, xc, g1c, lw["w_br_a"], lw["w_br_b"], lw["w_out"])
    xc = _moe(*_router(xc, lw["norm_ffn"], sc2c, sh2c, *rw), xc, g2c, *ew)
    return x, xc


def kernel(x, c, ctx, c_ctx, ada_w, ada_b, norm_mix, norm_ffn, w_in, q_norm, k_norm, rpb, short_w, short_b, flt_w1, flt_b1, flt_w2, flt_b2, flt_w3, hy_skip, w_br_a, w_br_b, w_out, w_group, b_group, w_router, b_router, moe_w1, moe_w3, moe_w2):
    B, L, D = x.shape
    Lc = ctx.shape[1]
    depth = ada_w.shape[0]
    consts = {"rope": _rope_tables(L), "dft": _dft_mats(L // 2), "dft_c": _dft_mats(Lc // 2)}
    rows = 8 * ((B + 1 + 7) // 8)
    cs = jnp.pad(jnp.concatenate([c, c_ctx[None, :]], axis=0), ((0, rows - B - 1), (0, 0)))
    xc = ctx
    for i in range(depth):
        mod = _ada(cs, ada_w[i], ada_b[i])
        mods = [m.reshape(B, 1, D) for m in jnp.split(mod[:B], 6, axis=-1)]
        modc = [jnp.broadcast_to(m.reshape(1, 1, D), (B, 1, D)) for m in jnp.split(mod[B], 6, axis=-1)]
        ef = EXPERTS_PER_GROUP * D_EXPERT
        lw = {
            "norm_mix": norm_mix[i], "norm_ffn": norm_ffn[i], "w_in": w_in[i].astype(BF16),
            "q_norm": q_norm[i], "k_norm": k_norm[i], "rpb": rpb[i],
            "short_w": short_w[i], "short_b": short_b[i],
            "flt_w1": flt_w1[i], "flt_b1": flt_b1[i], "flt_w2": flt_w2[i], "flt_b2": flt_b2[i], "flt_w3": flt_w3[i],
            "hy_skip": hy_skip[i],
            "w_br_a": w_br_a[i].astype(BF16), "w_br_b": w_br_b[i].astype(BF16), "w_out": w_out[i].astype(BF16),
            "w_group": w_group[i], "b_group": b_group[i], "w_router": w_router[i], "b_router": b_router[i],
            "moe_w1t": moe_w1[i].astype(BF16).transpose(0, 1, 3, 2).reshape(N_GROUPS, ef, D),
            "moe_w3t": moe_w3[i].astype(BF16).transpose(0, 1, 3, 2).reshape(N_GROUPS, ef, D),
            "moe_w2t": moe_w2[i].astype(BF16).transpose(0, 3, 1, 2).reshape(N_GROUPS, D, ef),
        }
        x, xc = _layer(x, xc, mods, modc, i == depth - 1, lw, consts)
    return x
```
